```python
import jax, jax.numpy as jnp
from jax import lax
import numpy as np

D_MODEL = 1024
BATCH = 8
SEQ = 2048
DEPTH = 2
DEC_BATCH = 32
DEC_SEQ = 64
PAST_LEN = 2048

CHUNK = 64
N_MIXERS = 2
N_SSD_LAYERS = (DEPTH + 1) // 2
N_CMLP_LAYERS = DEPTH // 2
SSD_D_INNER = 2 * D_MODEL
SSD_HEAD_DIM = 64
SSD_HEADS = SSD_D_INNER // SSD_HEAD_DIM
SSD_GROUPS = 8
SSD_STATE = 128
SSD_CONV = 4
SSD_CONV_DIM = SSD_D_INNER + 2 * SSD_GROUPS * SSD_STATE
SSD_IN_DIM = SSD_D_INNER + SSD_CONV_DIM + SSD_HEADS
CM_BLOCK = 128
CM_WIDTH = 2 * D_MODEL
CM_GROUPS = 8
FFN_HIDDEN = 4 * D_MODEL
DN_ALPHA = (2 * DEPTH) ** 0.25
DN_BETA = (8 * DEPTH) ** -0.25
LN_EPS = 1e-5

kernel_name = "hybrid_ssd_chunk_gmlp_streaming_step"


def layer_norm(x, g, b):
    xf = x.astype(jnp.float32)
    mu = jnp.mean(xf, axis=-1, keepdims=True)
    var = jnp.mean(jnp.square(xf - mu), axis=-1, keepdims=True)
    return ((xf - mu) * lax.rsqrt(var + LN_EPS) * g + b).astype(x.dtype)


def causal_conv_silu(xbc, prev, w, b):
    L = xbc.shape[1]
    full = jnp.concatenate([prev.astype(xbc.dtype), xbc], axis=1)
    acc = full[:, 0:L] * w[0]
    for k in range(1, SSD_CONV):
        acc = acc + full[:, k:k + L] * w[k]
    return jax.nn.silu(acc + b), full[:, -(SSD_CONV - 1):]


def ssd_scan(xh, dt, A, Bm, Cm, h0):
    b, L, H, P = xh.shape
    G, N = Bm.shape[2], Bm.shape[3]
    R = H // G
    Q = CHUNK if L % CHUNK == 0 else L
    nc = L // Q

    def to_blocks(t):
        return jnp.moveaxis(t.reshape((b, nc, Q) + t.shape[2:]), 1, 0)

    xdt = (xh.astype(jnp.float32) * dt[..., None]).reshape(b, L, G, R, P)
    dA = (dt * A).reshape(b, L, G, R)
    xs = (to_blocks(xdt), to_blocks(dA), to_blocks(Bm.astype(jnp.float32)), to_blocks(Cm.astype(jnp.float32)))
    causal = jnp.tril(jnp.ones((Q, Q), dtype=bool))[None, :, :, None, None]

    def step(h, inp):
        xc, dac, bc, cc = inp
        acum = jnp.cumsum(dac, axis=1)
        seg = acum[:, :, None] - acum[:, None, :]
        decay = jnp.exp(jnp.where(causal, seg, -jnp.inf))
        cb = jnp.einsum('btgn,bsgn->btsg', cc, bc)
        y_diag = jnp.einsum('btsg,btsgr,bsgrp->btgrp', cb, decay, xc)
        y_off = jnp.einsum('btgn,bgrpn,btgr->btgrp', cc, h, jnp.exp(acum))
        a_last = acum[:, -1]
        w_in = jnp.exp(a_last[:, None] - acum)
        h_new = h * jnp.exp(a_last)[..., None, None] + jnp.einsum('bsgn,bsgr,bsgrp->bgrpn', bc, w_in, xc)
        return h_new, y_diag + y_off

    hT, ys = lax.scan(step, h0.astype(jnp.float32).reshape(b, G, R, P, N), xs)
    y = jnp.moveaxis(ys, 0, 1).reshape(b, L, H, P)
    return y, hT.reshape(b, H, P, N)


def ssd_mixer(x, conv_prev, h0, w_in, conv_w, conv_b, dt_bias, a_log, d_skip, norm_w, w_out):
    b, L, _ = x.shape
    proj = x @ w_in
    z = proj[..., :SSD_D_INNER]
    xbc = proj[..., SSD_D_INNER:SSD_D_INNER + SSD_CONV_DIM]
    dt_raw = proj[..., SSD_D_INNER + SSD_CONV_DIM:]
    xbc, conv_new = causal_conv_silu(xbc, conv_prev, conv_w, conv_b)
    xs = xbc[..., :SSD_D_INNER].reshape(b, L, SSD_HEADS, SSD_HEAD_DIM)
    Bm = xbc[..., SSD_D_INNER:SSD_D_INNER + SSD_GROUPS * SSD_STATE].reshape(b, L, SSD_GROUPS, SSD_STATE)
    Cm = xbc[..., SSD_D_INNER + SSD_GROUPS * SSD_STATE:].reshape(b, L, SSD_GROUPS, SSD_STATE)
    dt = jax.nn.softplus(dt_raw.astype(jnp.float32) + dt_bias.astype(jnp.float32))
    A = -jnp.exp(a_log.astype(jnp.float32))
    y, hT = ssd_scan(xs, dt, A, Bm, Cm, h0)
    y = y + d_skip.astype(jnp.float32)[:, None] * xs.astype(jnp.float32)
    g = y.reshape(b, L, SSD_D_INNER) * jax.nn.silu(z.astype(jnp.float32))
    gg = g.reshape(b, L, SSD_GROUPS, SSD_D_INNER // SSD_GROUPS)
    gg = gg * lax.rsqrt(jnp.mean(jnp.square(gg), axis=-1, keepdims=True) + LN_EPS)
    normed = (gg.reshape(b, L, SSD_D_INNER) * norm_w).astype(x.dtype)
    return normed @ w_out, conv_new, hT.astype(h0.dtype)


def chunk_mlp_mixer(x, w_in, b_in, ln_g, ln_b, w_s, b_s, w_out):
    b, L, _ = x.shape
    h = jax.nn.gelu(x @ w_in + b_in)
    u, v = h[..., :CM_WIDTH], h[..., CM_WIDTH:]
    v = layer_norm(v, ln_g, ln_b)
    nb = L // CM_BLOCK if L % CM_BLOCK == 0 else 1
    blk = L // nb
    mask = jnp.tril(jnp.ones((blk, blk), dtype=bool))
    ws = jnp.where(mask, w_s[:, :blk, :blk], 0.0)
    vb = v.reshape(b, nb, blk, CM_GROUPS, CM_WIDTH // CM_GROUPS)
    s = jnp.einsum('gts,bnsgc->bntgc', ws, vb) + jnp.swapaxes(b_s[:, :blk], 0, 1)[None, None, :, :, None]
    out = u * s.reshape(b, L, CM_WIDTH)
    return out @ w_out, v


def sq_relu_ffn(x, w1, w2):
    return jnp.square(jax.nn.relu(x @ w1)) @ w2


def run_trunk(x, conv_prev, h0, ssd_w_in, ssd_conv_w, ssd_conv_b, ssd_dt_bias, ssd_a_log, ssd_d, ssd_norm_w, ssd_w_out,
              cm_w_in, cm_b_in, cm_ln_g, cm_ln_b, cm_w_s, cm_b_s, cm_w_out, ffn_w1, ffn_w2, ln1_g, ln1_b, ln2_g, ln2_b):
    conv_out, h_out, v_out = [], [], []
    for i in range(DEPTH):
        j = i // N_MIXERS
        if i % N_MIXERS == 0:
            mix, c_new, h_new = ssd_mixer(x, conv_prev[j], h0[j], ssd_w_in[j], ssd_conv_w[j], ssd_conv_b[j], ssd_dt_bias[j],
                                          ssd_a_log[j], ssd_d[j], ssd_norm_w[j], ssd_w_out[j])
            conv_out.append(c_new)
            h_out.append(h_new)
        else:
            mix, v_rows = chunk_mlp_mixer(x, cm_w_in[j], cm_b_in[j], cm_ln_g[j], cm_ln_b[j], cm_w_s[j], cm_b_s[j], cm_w_out[j])
            v_out.append(v_rows)
        x = layer_norm(DN_ALPHA * x + mix, ln1_g[i], ln1_b[i])
        x = layer_norm(DN_ALPHA * x + sq_relu_ffn(x, ffn_w1[i], ffn_w2[i]), ln2_g[i], ln2_b[i])
    return x, conv_out, h_out, v_out


def setup_inputs(seed: int = 0) -> dict:
    key = jax.random.key(seed)
    ks = jax.random.split(key, 32)
    f32 = jnp.float32

    def nrm(k, shape, scale):
        return jax.random.normal(k, shape, f32) * scale

    nS, nC = N_SSD_LAYERS, N_CMLP_LAYERS
    dt0 = jnp.exp(jax.random.uniform(ks[8], (nS, SSD_HEADS), f32, np.log(1e-3), np.log(1e-1)))
    dt_bias = dt0 + jnp.log(-jnp.expm1(-dt0))
    return {
        "x_prompt": nrm(ks[0], (BATCH, SEQ, D_MODEL), 1.0),
        "x_sample": nrm(ks[1], (DEC_BATCH, DEC_SEQ, D_MODEL), 1.0),
        "state_ssm": nrm(ks[2], (nS, DEC_BATCH, SSD_HEADS, SSD_HEAD_DIM, SSD_STATE), 0.5),
        "state_conv": nrm(ks[3], (nS, DEC_BATCH, SSD_CONV - 1, SSD_CONV_DIM), 1.0),
        "ssd_w_in": nrm(ks[4], (nS, D_MODEL, SSD_IN_DIM), D_MODEL ** -0.5),
        "ssd_conv_w": nrm(ks[5], (nS, SSD_CONV, SSD_CONV_DIM), SSD_CONV ** -0.5),
        "ssd_conv_b": nrm(ks[6], (nS, SSD_CONV_DIM), 0.02),
        "ssd_dt_bias": dt_bias,
        "ssd_a_log": jnp.log(jax.random.uniform(ks[9], (nS, SSD_HEADS), f32, 1.0, 16.0)),
        "ssd_d": 1.0 + nrm(ks[10], (nS, SSD_HEADS), 0.1),
        "ssd_norm_w": 1.0 + nrm(ks[11], (nS, SSD_D_INNER), 0.02),
        "ssd_w_out": nrm(ks[12], (nS, SSD_D_INNER, D_MODEL), SSD_D_INNER ** -0.5 * DN_BETA),
        "cm_w_in": nrm(ks[13], (nC, D_MODEL, 2 * CM_WIDTH), D_MODEL ** -0.5),
        "cm_b_in": nrm(ks[14], (nC, 2 * CM_WIDTH), 0.02),
        "cm_ln_g": 1.0 + nrm(ks[15], (nC, CM_WIDTH), 0.02),
        "cm_ln_b": nrm(ks[16], (nC, CM_WIDTH), 0.02),
        "cm_w_s": nrm(ks[17], (nC, CM_GROUPS, CM_BLOCK, CM_BLOCK), CM_BLOCK ** -0.5),
        "cm_b_s": 1.0 + nrm(ks[18], (nC, CM_GROUPS, CM_BLOCK), 0.1),
        "cm_w_out": nrm(ks[19], (nC, CM_WIDTH, D_MODEL), CM_WIDTH ** -0.5 * DN_BETA),
        "ffn_w1": nrm(ks[20], (DEPTH, D_MODEL, FFN_HIDDEN), D_MODEL ** -0.5),
        "ffn_w2": nrm(ks[21], (DEPTH, FFN_HIDDEN, D_MODEL), FFN_HIDDEN ** -0.5 * DN_BETA),
        "ln1_g": 1.0 + nrm(ks[22], (DEPTH, D_MODEL), 0.02),
        "ln1_b": nrm(ks[23], (DEPTH, D_MODEL), 0.02),
        "ln2_g": 1.0 + nrm(ks[24], (DEPTH, D_MODEL), 0.02),
        "ln2_b": nrm(ks[25], (DEPTH, D_MODEL), 0.02),
    }


def reference(x_prompt, x_sample, state_ssm, state_conv, ssd_w_in, ssd_conv_w, ssd_conv_b, ssd_dt_bias, ssd_a_log, ssd_d,
              ssd_norm_w, ssd_w_out, cm_w_in, cm_b_in, cm_ln_g, cm_ln_b, cm_w_s, cm_b_s, cm_w_out, ffn_w1, ffn_w2,
              ln1_g, ln1_b, ln2_g, ln2_b):
    weights = (ssd_w_in, ssd_conv_w, ssd_conv_b, ssd_dt_bias, ssd_a_log, ssd_d, ssd_norm_w, ssd_w_out,
               cm_w_in, cm_b_in, cm_ln_g, cm_ln_b, cm_w_s, cm_b_s, cm_w_out, ffn_w1, ffn_w2, ln1_g, ln1_b, ln2_g, ln2_b)
    b_p = x_prompt.shape[0]
    conv0 = jnp.zeros((N_SSD_LAYERS, b_p, SSD_CONV - 1, SSD_CONV_DIM), x_prompt.dtype)
    h00 = jnp.zeros((N_SSD_LAYERS, b_p, SSD_HEADS, SSD_HEAD_DIM, SSD_STATE), x_prompt.dtype)
    y_prompt, conv_p, h_p, _ = run_trunk(x_prompt, conv0, h00, *weights)
    y_sample, conv_s, h_s, v_s = run_trunk(x_sample, state_conv, state_ssm, *weights)
    ssm_prompt = jnp.stack(h_p)
    conv_prompt = jnp.stack(conv_p)
    ssm_sample = jnp.stack(h_s)
    conv_sample = jnp.stack(conv_s)
    v_rows_sample = jnp.stack(v_s)
    return (y_prompt, y_sample, ssm_prompt, conv_prompt, ssm_sample, conv_sample, v_rows_sample)
```

```python
import functools

import numpy as np
import jax
import jax.numpy as jnp
from jax import lax
from jax.experimental import pallas as pl
from jax.experimental.pallas import tpu as pltpu

F32 = jnp.float32
BF16 = jnp.bfloat16

D_MODEL = 1024
DEPTH = 2
CHUNK = 64
SSD_D_INNER = 2 * D_MODEL
SSD_HEAD_DIM = 64
SSD_HEADS = SSD_D_INNER // SSD_HEAD_DIM
SSD_GROUPS = 8
SSD_STATE = 128
SSD_CONV = 4
SSD_BC = SSD_GROUPS * SSD_STATE
SSD_CONV_DIM = SSD_D_INNER + 2 * SSD_BC
HEADS_PER_GROUP = SSD_HEADS // SSD_GROUPS
GROUP_W = HEADS_PER_GROUP * SSD_HEAD_DIM
NORM_GROUP_W = SSD_D_INNER // SSD_GROUPS
CM_BLOCK = 128
CM_WIDTH = 2 * D_MODEL
CM_GROUPS = 8
CM_GROUP_W = CM_WIDTH // CM_GROUPS
FFN_HIDDEN = 4 * D_MODEL
DN_ALPHA = (2 * DEPTH) ** 0.25
LN_EPS = 1e-5

LANES = 128
CONV_PAD = 8
VMEM_LIMIT = 56 * 1024 * 1024

HEAD_PERM = np.concatenate([np.arange(0, SSD_HEADS, 2), np.arange(1, SSD_HEADS, 2)])


def _expand_matrix():
    e = np.zeros((2 * LANES, SSD_D_INNER), np.float32)
    for k, h in enumerate(HEAD_PERM):
        e[k, h * SSD_HEAD_DIM:(h + 1) * SSD_HEAD_DIM] = 1.0
        e[LANES + k, h * SSD_HEAD_DIM:(h + 1) * SSD_HEAD_DIM] = 1.0
    return e


def _layer_norm(x, g, b):
    mu = jnp.mean(x, axis=-1, keepdims=True)
    xc = x - mu
    var = jnp.mean(xc * xc, axis=-1, keepdims=True)
    return xc * lax.rsqrt(var + LN_EPS) * g + b


def _sigmoid(x):
    return 1.0 / (1.0 + jnp.exp(-x))


def _split2(v):
    hi = v.astype(BF16)
    lo = (v - hi.astype(F32)).astype(BF16)
    return jnp.concatenate([hi, lo], axis=1)


def _const_spec(shape):
    nd = len(shape)
    return pl.BlockSpec(shape, lambda *_: (0,) * nd, pipeline_mode=pl.Buffered(1))


def _ssd_kernel(*refs, sb_n, tt, has_state):
    if has_state:
        (x_ref, cprev_ref, h0_ref, *refs) = refs
    else:
        (x_ref, *refs) = refs
    (wz_ref, wdt_ref, cw_ref, cb_ref, dtb_ref, alog_ref, dexp_ref, nw_ref, wout_ref, e2_ref,
     g_ref, b_ref, o_ref, cnew_ref, hout_ref,
     z_s, ext_s, act_s, dtx_s, acx_s, xdt_s, y_s, ac_s, p_s, h_s) = refs
    rows = sb_n * tt
    nch = tt // CHUNK
    t = pl.program_id(1)

    @pl.when(t == 0)
    def _init():
        ext_s[:, 0:CONV_PAD, :] = jnp.zeros((sb_n, CONV_PAD, SSD_CONV_DIM), F32)
        if has_state:
            for sb in range(sb_n):
                ext_s[sb, CONV_PAD - (SSD_CONV - 1):CONV_PAD, :] = cprev_ref[sb]
                for g in range(SSD_GROUPS):
                    hg = h0_ref[sb, g * HEADS_PER_GROUP:(g + 1) * HEADS_PER_GROUP]
                    h_s[sb, g] = hg.reshape(GROUP_W, SSD_STATE).T
        else:
            h_s[...] = jnp.zeros(h_s.shape, F32)

    x = x_ref[...].reshape(rows, D_MODEL)
    xb = x.astype(BF16)
    z_s[...] = jnp.dot(xb, wz_ref[:, 0:SSD_D_INNER], preferred_element_type=F32)
    xbc = jnp.dot(xb, wz_ref[:, SSD_D_INNER:], preferred_element_type=F32)
    for sb in range(sb_n):
        ext_s[sb, CONV_PAD:CONV_PAD + tt, :] = xbc[sb * tt:(sb + 1) * tt]
    dt_raw = jnp.dot(xb, wdt_ref[...], preferred_element_type=F32)

    lane_blk = 512
    for sb in range(sb_n):
        for c0 in range(0, SSD_CONV_DIM, lane_blk):
            acc = cb_ref[:, c0:c0 + lane_blk]
            for k in range(SSD_CONV):
                win = ext_s[sb, CONV_PAD - (SSD_CONV - 1) + k:CONV_PAD - (SSD_CONV - 1) + k + tt, c0:c0 + lane_blk]
                acc = acc + win * cw_ref[k:k + 1, c0:c0 + lane_blk]
            act_s[sb * tt:(sb + 1) * tt, c0:c0 + lane_blk] = acc * _sigmoid(acc)
    tail = ext_s[:, tt + CONV_PAD - (SSD_CONV - 1):tt + CONV_PAD, :]
    ext_s[:, CONV_PAD - (SSD_CONV - 1):CONV_PAD, :] = tail

    @pl.when(t == pl.num_programs(1) - 1)
    def _conv_out():
        cnew_ref[...] = tail

    v = dt_raw + dtb_ref[...]
    dt = jnp.maximum(v, 0.0) + jnp.log1p(jnp.exp(-jnp.abs(v)))
    a_neg = -jnp.exp(alog_ref[...])
    d_a = dt * a_neg
    ri = lax.broadcasted_iota(jnp.int32, (rows, rows), 0)
    ci = lax.broadcasted_iota(jnp.int32, (rows, rows), 1)
    same_chunk = (ri & -CHUNK) == (ci & -CHUNK)
    tril = jnp.where((ci <= ri) & same_chunk, 1.0, 0.0).astype(BF16)
    cs = jnp.dot(tril, _split2(d_a), preferred_element_type=F32)
    acum = cs[:, 0:LANES] + cs[:, LANES:]
    ac_s[...] = acum
    dtx_s[...] = jnp.dot(_split2(dt), e2_ref[...], preferred_element_type=F32)
    acx_s[...] = jnp.dot(_split2(acum), e2_ref[...], preferred_element_type=F32)
    xdt_s[...] = act_s[:, 0:SSD_D_INNER] * dtx_s[...]

    row_i = lax.broadcasted_iota(jnp.int32, (CHUNK, LANES), 0)
    lane_i = lax.broadcasted_iota(jnp.int32, (CHUNK, LANES), 1)
    causal2 = (lane_i & (CHUNK - 1)) <= row_i
    first_half = lane_i < CHUNK
    n_pairs = SSD_HEADS // 2
    for chunk in range(sb_n * nch):
        sb = chunk // nch
        r0 = chunk * CHUNK
        rs = slice(r0, r0 + CHUNK)
        ac_t = ac_s[rs, :].T
        p_s[...] = jnp.concatenate([ac_t[0:n_pairs], ac_t[n_pairs:2 * n_pairs]], axis=1)
        for g in range(SSD_GROUPS):
            gs = slice(g * GROUP_W, (g + 1) * GROUP_W)
            b_g = act_s[rs, SSD_D_INNER + g * SSD_STATE:SSD_D_INNER + (g + 1) * SSD_STATE].astype(BF16)
            c_g = act_s[rs, SSD_D_INNER + SSD_BC + g * SSD_STATE:SSD_D_INNER + SSD_BC + (g + 1) * SSD_STATE].astype(BF16)
            bb = jnp.concatenate([b_g, b_g], axis=0)
            cb2 = lax.dot_general(c_g, bb, (((1,), (1,)), ((), ())), preferred_element_type=F32)
            h_g = h_s[sb, g]
            acx_g = acx_s[rs, gs]
            y_off = jnp.dot(c_g, h_g.astype(BF16), preferred_element_type=F32) * jnp.exp(acx_g)
            for jj in range(2):
                j = 2 * g + jj
                ls = slice(j * LANES, (j + 1) * LANES)
                seg = acx_s[rs, ls] - p_s[j:j + 1, :]
                decay = jnp.where(causal2, jnp.exp(seg), 0.0)
                m2 = (cb2 * decay).astype(BF16)
                xp = xdt_s[rs, ls]
                xbd = jnp.concatenate([jnp.where(first_half, xp, 0.0), jnp.where(first_half, 0.0, xp)], axis=0)
                y_diag = jnp.dot(m2, xbd.astype(BF16), preferred_element_type=F32)
                y_s[rs, ls] = y_diag + y_off[:, jj * LANES:(jj + 1) * LANES]
            a_last = acx_s[r0 + CHUNK - 1:r0 + CHUNK, gs]
            xw = (xdt_s[rs, gs] * jnp.exp(a_last - acx_g)).astype(BF16)
            upd = lax.dot_general(b_g, xw, (((0,), (0,)), ((), ())), preferred_element_type=F32)
            h_s[sb, g] = h_g * jnp.exp(a_last) + upd

    @pl.when(t == pl.num_programs(1) - 1)
    def _state_out():
        for sb in range(sb_n):
            for g in range(SSD_GROUPS):
                hout_ref[sb, g * HEADS_PER_GROUP:(g + 1) * HEADS_PER_GROUP] = (
                    h_s[sb, g].T.reshape(HEADS_PER_GROUP, SSD_HEAD_DIM, SSD_STATE))

    for g in range(SSD_GROUPS):
        gs = slice(g * NORM_GROUP_W, (g + 1) * NORM_GROUP_W)
        y = y_s[:, gs] + dexp_ref[:, gs] * act_s[:, gs]
        zg = z_s[:, gs]
        gg = y * (zg * _sigmoid(zg))
        ms = jnp.mean(gg * gg, axis=-1, keepdims=True)
        y_s[:, gs] = gg * lax.rsqrt(ms + LN_EPS) * nw_ref[:, gs]
    mix = jnp.dot(y_s[...].astype(BF16), wout_ref[...], preferred_element_type=F32)
    o_ref[...] = _layer_norm(DN_ALPHA * x + mix, g_ref[...], b_ref[...]).reshape(sb_n, tt, D_MODEL)


def _ssd_layer(x, conv_prev, h0, w, *, sb_n, tt):
    nseq, seq_len, _ = x.shape
    has_state = h0 is not None
    rows = sb_n * tt
    grid = (nseq // sb_n, seq_len // tt)
    seq_spec = pl.BlockSpec((sb_n, tt, D_MODEL), lambda b, t: (b, t, 0))
    conv_spec = pl.BlockSpec((sb_n, SSD_CONV - 1, SSD_CONV_DIM), lambda b, t: (b, 0, 0))
    state_spec = pl.BlockSpec((sb_n, SSD_HEADS, SSD_HEAD_DIM, SSD_STATE), lambda b, t: (b, 0, 0, 0))
    consts = (w["w_zx"], w["w_dt"], w["conv_w"], w["conv_b"], w["dt_bias"], w["a_log"], w["d_exp"],
              w["norm_w"], w["w_out"], w["e2"], w["ln_g"], w["ln_b"])
    in_specs = [seq_spec] + ([conv_spec, state_spec] if has_state else []) + [_const_spec(c.shape) for c in consts]
    args = (x,) + ((conv_prev, h0) if has_state else ()) + consts
    scratch = [
        pltpu.VMEM((rows, SSD_D_INNER), F32),
        pltpu.VMEM((sb_n, tt + CONV_PAD, SSD_CONV_DIM), F32),
        pltpu.VMEM((rows, SSD_CONV_DIM), F32),
        pltpu.VMEM((rows, SSD_D_INNER), F32),
        pltpu.VMEM((rows, SSD_D_INNER), F32),
        pltpu.VMEM((rows, SSD_D_INNER), F32),
        pltpu.VMEM((rows, SSD_D_INNER), F32),
        pltpu.VMEM((rows, LANES), F32),
        pltpu.VMEM((SSD_HEADS // 2, LANES), F32),
        pltpu.VMEM((sb_n, SSD_GROUPS, SSD_STATE, GROUP_W), F32),
    ]
    out_shape = (
        jax.ShapeDtypeStruct(x.shape, F32),
        jax.ShapeDtypeStruct((nseq, SSD_CONV - 1, SSD_CONV_DIM), F32),
        jax.ShapeDtypeStruct((nseq, SSD_HEADS, SSD_HEAD_DIM, SSD_STATE), F32),
    )
    return pl.pallas_call(
        functools.partial(_ssd_kernel, sb_n=sb_n, tt=tt, has_state=has_state),
        grid=grid,
        in_specs=in_specs,
        out_specs=(seq_spec, conv_spec, state_spec),
        out_shape=out_shape,
        scratch_shapes=scratch,
        compiler_params=pltpu.CompilerParams(
            dimension_semantics=("arbitrary", "arbitrary"), vmem_limit_bytes=VMEM_LIMIT),
        name="ssd_state" if has_state else "ssd_fresh",
    )(*args)


def _ffn_kernel(x_ref, w1_ref, w2_ref, g_ref, b_ref, o_ref):
    x = x_ref[...]
    h = jnp.dot(x.astype(BF16), w1_ref[...], preferred_element_type=F32)
    h = jnp.maximum(h, 0.0)
    y = jnp.dot((h * h).astype(BF16), w2_ref[...], preferred_element_type=F32)
    o_ref[...] = _layer_norm(DN_ALPHA * x + y, g_ref[...], b_ref[...])


def _ffn_layer(x2d, w1, w2, g, b, *, tm):
    n = x2d.shape[0]
    row_spec = pl.BlockSpec((tm, D_MODEL), lambda i: (i, 0))
    return pl.pallas_call(
        _ffn_kernel,
        grid=(n // tm,),
        in_specs=[row_spec, _const_spec(w1.shape), _const_spec(w2.shape), _const_spec(g.shape), _const_spec(b.shape)],
        out_specs=row_spec,
        out_shape=jax.ShapeDtypeStruct(x2d.shape, F32),
        compiler_params=pltpu.CompilerParams(dimension_semantics=("arbitrary",), vmem_limit_bytes=VMEM_LIMIT),
        name="ffn",
    )(x2d, w1, w2, g, b)


def _cmlp_kernel(x_ref, win_ref, bin_ref, lng_ref, lnb_ref, ws_ref, bs_ref, wout_ref, g_ref, b_ref,
                 o_ref, *rest, blk, emit_v):
    if emit_v:
        v_ref, u_s, v_s = rest
    else:
        u_s, v_s = rest
    x = x_ref[...]
    tm = x.shape[0]
    h = jnp.dot(x.astype(BF16), win_ref[...], preferred_element_type=F32) + bin_ref[...]
    h = 0.5 * h * (1.0 + jnp.tanh(0.7978845608028654 * (h + 0.044715 * (h * h * h))))
    u_s[...] = h[:, 0:CM_WIDTH]
    v = _layer_norm(h[:, CM_WIDTH:], lng_ref[...], lnb_ref[...])
    if emit_v:
        v_ref[...] = v
    v_s[...] = v.astype(BF16)
    ri = lax.broadcasted_iota(jnp.int32, (blk, blk), 0)
    ci = lax.broadcasted_iota(jnp.int32, (blk, blk), 1)
    for g in range(CM_GROUPS):
        gs = slice(g * CM_GROUP_W, (g + 1) * CM_GROUP_W)
        ws_g = jnp.where(ci <= ri, ws_ref[g, 0:blk, 0:blk], 0.0).astype(BF16)
        for i in range(tm // blk):
            rs = slice(i * blk, (i + 1) * blk)
            s = jnp.dot(ws_g, v_s[rs, gs], preferred_element_type=F32) + bs_ref[0:blk, gs]
            u_s[rs, gs] = u_s[rs, gs] * s
    mix = jnp.dot(u_s[...].astype(BF16), wout_ref[...], preferred_element_type=F32)
    o_ref[...] = _layer_norm(DN_ALPHA * x + mix, g_ref[...], b_ref[...])


def _cmlp_layer(x2d, w, *, tm, blk, emit_v):
    n = x2d.shape[0]
    row_spec = pl.BlockSpec((tm, D_MODEL), lambda i: (i, 0))
    consts = (w["w_in"], w["b_in"], w["ln_g"], w["ln_b"], w["w_s"], w["b_s"], w["w_out"], w["ln1_g"], w["ln1_b"])
    out_shape = [jax.ShapeDtypeStruct(x2d.shape, F32)]
    out_specs = [row_spec]
    if emit_v:
        out_shape.append(jax.ShapeDtypeStruct((n, CM_WIDTH), F32))
        out_specs.append(pl.BlockSpec((tm, CM_WIDTH), lambda i: (i, 0)))
    return pl.pallas_call(
        functools.partial(_cmlp_kernel, blk=blk, emit_v=emit_v),
        grid=(n // tm,),
        in_specs=[row_spec] + [_const_spec(c.shape) for c in consts],
        out_specs=out_specs,
        out_shape=out_shape,
        scratch_shapes=[pltpu.VMEM((tm, CM_WIDTH), F32), pltpu.VMEM((tm, CM_WIDTH), BF16)],
        compiler_params=pltpu.CompilerParams(dimension_semantics=("arbitrary",), vmem_limit_bytes=VMEM_LIMIT),
        name="cmlp_v" if emit_v else "cmlp",
    )(x2d, *consts)


def kernel(x_prompt, x_sample, state_ssm, state_conv, ssd_w_in, ssd_conv_w, ssd_conv_b, ssd_dt_bias, ssd_a_log, ssd_d, ssd_norm_w, ssd_w_out, cm_w_in, cm_b_in, cm_ln_g, cm_ln_b, cm_w_s, cm_b_s, cm_w_out, ffn_w1, ffn_w2, ln1_g, ln1_b, ln2_g, ln2_b):
    row = lambda a: a.reshape(1, -1)
    perm = jnp.asarray(HEAD_PERM)

    def head_lanes(a):
        return jnp.pad(a[perm], (0, LANES - SSD_HEADS)).reshape(1, LANES)

    w_in = ssd_w_in[0]
    zx_end = SSD_D_INNER + SSD_CONV_DIM
    ssd_w = {
        "w_zx": w_in[:, 0:zx_end].astype(BF16),
        "w_dt": jnp.pad(w_in[:, zx_end:][:, perm], ((0, 0), (0, LANES - SSD_HEADS))).astype(BF16),
        "conv_w": ssd_conv_w[0],
        "conv_b": row(ssd_conv_b[0]),
        "dt_bias": head_lanes(ssd_dt_bias[0]),
        "a_log": head_lanes(ssd_a_log[0]),
        "d_exp": row(jnp.repeat(ssd_d[0], SSD_HEAD_DIM)),
        "norm_w": row(ssd_norm_w[0]),
        "w_out": ssd_w_out[0].astype(BF16),
        "e2": jnp.asarray(_expand_matrix(), BF16),
        "ln_g": row(ln1_g[0]),
        "ln_b": row(ln1_b[0]),
    }
    cm_w = {
        "w_in": cm_w_in[0].astype(BF16),
        "b_in": row(cm_b_in[0]),
        "ln_g": row(cm_ln_g[0]),
        "ln_b": row(cm_ln_b[0]),
        "w_s": cm_w_s[0],
        "b_s": jnp.repeat(cm_b_s[0].T, CM_GROUP_W, axis=1),
        "w_out": cm_w_out[0].astype(BF16),
        "ln1_g": row(ln1_g[1]),
        "ln1_b": row(ln1_b[1]),
    }
    ffn = [(ffn_w1[i].astype(BF16), ffn_w2[i].astype(BF16), row(ln2_g[i]), row(ln2_b[i])) for i in range(DEPTH)]

    def trunk(x, conv_prev, h0, *, sb_n, tt, blk, emit_v):
        shape = x.shape
        x1, conv_new, h_new = _ssd_layer(x, conv_prev, h0, ssd_w, sb_n=sb_n, tt=tt)
        x2 = _ffn_layer(x1.reshape(-1, D_MODEL), *ffn[0], tm=512)
        res = _cmlp_layer(x2, cm_w, tm=512, blk=blk, emit_v=emit_v)
        x4 = _ffn_layer(res[0], *ffn[1], tm=512)
        v_rows = res[1].reshape(shape[0], shape[1], CM_WIDTH) if emit_v else None
        return x4.reshape(shape), conv_new, h_new, v_rows

    y_p, conv_p, h_p, _ = trunk(x_prompt, None, None, sb_n=1, tt=256, blk=CM_BLOCK, emit_v=False)
    y_s, conv_s, h_s, v_s = trunk(x_sample, state_conv[0], state_ssm[0], sb_n=2, tt=CHUNK,
                                  blk=x_sample.shape[1], emit_v=True)
    return (y_p, y_s, h_p[None], conv_p[None], h_s[None], conv_s[None], v_s[None])
```

```python
import functools

import numpy as np
import jax
import jax.numpy as jnp
from jax import lax
from jax.experimental import pallas as pl
from jax.experimental.pallas import tpu as pltpu

F32 = jnp.float32
BF16 = jnp.bfloat16

D_MODEL = 1024
DEPTH = 2
CHUNK = 64
SSD_D_INNER = 2 * D_MODEL
SSD_HEAD_DIM = 64
SSD_HEADS = SSD_D_INNER // SSD_HEAD_DIM
SSD_GROUPS = 8
SSD_STATE = 128
SSD_CONV = 4
SSD_BC = SSD_GROUPS * SSD_STATE
SSD_CONV_DIM = SSD_D_INNER + 2 * SSD_BC
HEADS_PER_GROUP = SSD_HEADS // SSD_GROUPS
GROUP_W = HEADS_PER_GROUP * SSD_HEAD_DIM
NORM_GROUP_W = SSD_D_INNER // SSD_GROUPS
CM_BLOCK = 128
CM_WIDTH = 2 * D_MODEL
CM_GROUPS = 8
CM_GROUP_W = CM_WIDTH // CM_GROUPS
FFN_HIDDEN = 4 * D_MODEL
DN_ALPHA = (2 * DEPTH) ** 0.25
LN_EPS = 1e-5

LANES = 128
CONV_PAD = 8
COL_BLK = 256
OUT_BLK = 256
COL_BLK_CM = 512
VMEM_LIMIT = 56 * 1024 * 1024

HEAD_PERM = np.concatenate([np.arange(0, SSD_HEADS, 2), np.arange(1, SSD_HEADS, 2)])


def _expand_matrix():
    e = np.zeros((2 * LANES, SSD_D_INNER), np.float32)
    for k, h in enumerate(HEAD_PERM):
        e[k, h * SSD_HEAD_DIM:(h + 1) * SSD_HEAD_DIM] = 1.0
        e[LANES + k, h * SSD_HEAD_DIM:(h + 1) * SSD_HEAD_DIM] = 1.0
    return e


def _layer_norm(x, g, b):
    mu = jnp.mean(x, axis=-1, keepdims=True)
    xc = x - mu
    var = jnp.mean(xc * xc, axis=-1, keepdims=True)
    return xc * lax.rsqrt(var + LN_EPS) * g + b


def _sigmoid(x):
    return 1.0 / (1.0 + jnp.exp(-x))


def _split2(v):
    hi = v.astype(BF16)
    lo = (v - hi.astype(F32)).astype(BF16)
    return jnp.concatenate([hi, lo], axis=1)


def _const_spec(shape):
    nd = len(shape)
    return pl.BlockSpec(shape, lambda *_: (0,) * nd, pipeline_mode=pl.Buffered(1))


def _ssd_stage_a(x_ref, wz_ref, wdt_ref, cw_ref, cb_ref, ext_s, handoff, *, sb_n, tt):
    z_w, xs_w, bc_w, dtr_w, xc_w = handoff
    rows = sb_n * tt
    hist = CONV_PAD - (SSD_CONV - 1)
    live = {}

    def first():
        x = x_ref[...].reshape(rows, D_MODEL)
        xc_w[...] = x
        live["xb"] = x.astype(BF16)
        dtr_w[...] = jnp.dot(live["xb"], wdt_ref[...], preferred_element_type=F32)

    def dot_unit(c0):
        cs = slice(c0, c0 + COL_BLK)
        raw = jnp.dot(live["xb"], wz_ref[:, SSD_D_INNER + c0:SSD_D_INNER + c0 + COL_BLK],
                      preferred_element_type=F32)
        for sb in range(sb_n):
            ext_s[sb, CONV_PAD:CONV_PAD + tt, cs] = raw[sb * tt:(sb + 1) * tt]

    def conv_unit(c0):
        cs = slice(c0, c0 + COL_BLK)
        for sb in range(sb_n):
            ext = ext_s[sb, :, cs]
            acc = cb_ref[:, cs] + ext[CONV_PAD:] * cw_ref[SSD_CONV - 1:SSD_CONV, cs]
            for k in range(SSD_CONV - 1):
                back = SSD_CONV - 1 - k
                acc = acc + pltpu.roll(ext, back, axis=0)[CONV_PAD:] * cw_ref[k:k + 1, cs]
            act = acc * _sigmoid(acc)
            rs = slice(sb * tt, (sb + 1) * tt)
            if c0 < SSD_D_INNER:
                xs_w[rs, cs] = act
            else:
                bc_w[rs, c0 - SSD_D_INNER:c0 - SSD_D_INNER + COL_BLK] = act.astype(BF16)
            ext_s[sb, hist:CONV_PAD, cs] = ext[tt + hist:tt + CONV_PAD]

    def z_unit(c0):
        z_w[:, c0:c0 + COL_BLK] = jnp.dot(live["xb"], wz_ref[:, c0:c0 + COL_BLK], preferred_element_type=F32)

    blocks = range(0, SSD_CONV_DIM, COL_BLK)
    return (first,
            [functools.partial(dot_unit, c0) for c0 in blocks],
            [functools.partial(conv_unit, c0) for c0 in blocks],
            [functools.partial(z_unit, c0) for c0 in range(0, SSD_D_INNER, COL_BLK)])


def _ssd_stage_b(handoff, dtb_ref, alog_ref, dexp_ref, nw_ref, wout_ref, e2_ref, g_ref, b_ref, o_ref,
                 dtx_s, acx_s, y_s, yb_s, ac_s, h_s, *, sb_n, tt):
    z_r, xs_r, bc_r, dtr_r, xc_r = handoff
    rows = sb_n * tt
    nch = tt // CHUNK
    n_pairs = SSD_HEADS // 2
    live = {}

    def prologue():
        v = dtr_r[...] + dtb_ref[...]
        dt = jnp.maximum(v, 0.0) + jnp.log1p(jnp.exp(-jnp.abs(v)))
        a_neg = -jnp.exp(alog_ref[...])
        d_a = dt * a_neg
        ri = lax.broadcasted_iota(jnp.int32, (rows, rows), 0)
        ci = lax.broadcasted_iota(jnp.int32, (rows, rows), 1)
        same_chunk = (ri & -CHUNK) == (ci & -CHUNK)
        tril = jnp.where((ci <= ri) & same_chunk, 1.0, 0.0).astype(BF16)
        cs = jnp.dot(tril, _split2(d_a), preferred_element_type=F32)
        acum = cs[:, 0:LANES] + cs[:, LANES:]
        ac_s[...] = acum
        acx_s[...] = jnp.dot(_split2(acum), e2_ref[...], preferred_element_type=F32)
        dtx_s[...] = xs_r[...] * jnp.dot(_split2(dt), e2_ref[...], preferred_element_type=F32)
        row_i = lax.broadcasted_iota(jnp.int32, (CHUNK, LANES), 0)
        lane_i = lax.broadcasted_iota(jnp.int32, (CHUNK, LANES), 1)
        live["causal2"] = (lane_i & (CHUNK - 1)) <= row_i
        live["first_half"] = lane_i < CHUNK

    def scan_unit(chunk, g):
        sb = chunk // nch
        r0 = chunk * CHUNK
        rs = slice(r0, r0 + CHUNK)
        if g == 0:
            ac_t = ac_s[rs, :].T
            live["pair_rows"] = jnp.concatenate([ac_t[0:n_pairs], ac_t[n_pairs:2 * n_pairs]], axis=1)
        pair_rows, causal2, first_half = live["pair_rows"], live["causal2"], live["first_half"]
        gs = slice(g * GROUP_W, (g + 1) * GROUP_W)
        b_g = bc_r[rs, g * SSD_STATE:(g + 1) * SSD_STATE]
        c_g = bc_r[rs, SSD_BC + g * SSD_STATE:SSD_BC + (g + 1) * SSD_STATE]
        bb = jnp.concatenate([b_g, b_g], axis=0)
        cb2 = lax.dot_general(c_g, bb, (((1,), (1,)), ((), ())), preferred_element_type=F32)
        h_g = h_s[sb, g]
        acx_g = acx_s[rs, gs]
        y_off = jnp.dot(c_g, h_g.astype(BF16), preferred_element_type=F32) * jnp.exp(acx_g)
        for jj in range(2):
            j = 2 * g + jj
            ls = slice(j * LANES, (j + 1) * LANES)
            seg = acx_s[rs, ls] - pair_rows[j:j + 1, :]
            decay = jnp.where(causal2, jnp.exp(seg), 0.0)
            m2 = (cb2 * decay).astype(BF16)
            xp = dtx_s[rs, ls]
            xbd = jnp.concatenate([jnp.where(first_half, xp, 0.0), jnp.where(first_half, 0.0, xp)], axis=0)
            y_diag = jnp.dot(m2, xbd.astype(BF16), preferred_element_type=F32)
            y_s[rs, ls] = y_diag + y_off[:, jj * LANES:(jj + 1) * LANES]
        a_last = acx_s[r0 + CHUNK - 1:r0 + CHUNK, gs]
        xw = (dtx_s[rs, gs] * jnp.exp(a_last - acx_g)).astype(BF16)
        upd = lax.dot_general(b_g, xw, (((0,), (0,)), ((), ())), preferred_element_type=F32)
        h_s[sb, g] = h_g * jnp.exp(a_last) + upd

    def gate_unit(g):
        gs = slice(g * NORM_GROUP_W, (g + 1) * NORM_GROUP_W)
        y = y_s[:, gs] + dexp_ref[:, gs] * xs_r[:, gs]
        zg = z_r[:, gs]
        gg = y * (zg * _sigmoid(zg))
        ms = jnp.mean(gg * gg, axis=-1, keepdims=True)
        yb_s[:, gs] = (gg * lax.rsqrt(ms + LN_EPS) * nw_ref[:, gs]).astype(BF16)

    def out_unit(c0):
        mix = jnp.dot(yb_s[...], wout_ref[:, c0:c0 + OUT_BLK], preferred_element_type=F32)
        o_ref[:, :, c0:c0 + OUT_BLK] = mix.reshape(sb_n, tt, OUT_BLK)

    def last():
        mix = o_ref[...].reshape(rows, D_MODEL)
        o_ref[...] = _layer_norm(DN_ALPHA * xc_r[...] + mix, g_ref[...], b_ref[...]).reshape(sb_n, tt, D_MODEL)

    return (prologue,
            [functools.partial(scan_unit, c, g) for c in range(sb_n * nch) for g in range(SSD_GROUPS)],
            [functools.partial(gate_unit, g) for g in range(SSD_GROUPS)],
            [functools.partial(out_unit, c0) for c0 in range(0, D_MODEL, OUT_BLK)],
            last)


def _ssd_trace_order(stage_a, stage_b):
    a_first, a_dots, a_convs, a_z = stage_a
    b_pro, b_scans, b_gates, b_outs, b_last = stage_b
    lead = 4
    n_blk = len(a_dots)
    order = [a_first, a_dots[0], a_dots[1], b_pro, a_dots[2], a_dots[3]]
    nxt = 0

    def conv_and_dot():
        nonlocal nxt
        order.append(a_convs[nxt])
        if nxt + lead < n_blk:
            order.append(a_dots[nxt + lead])
        nxt += 1

    for u, scan in enumerate(b_scans):
        order.append(scan)
        if u % 3 == 2 and nxt + lead < n_blk:
            conv_and_dot()
    for g, gate in enumerate(b_gates):
        order.append(gate)
        order.extend(a_z[g:g + 1])
        if nxt + lead < n_blk:
            conv_and_dot()
    order.extend(a_z[len(b_gates):])
    for out in b_outs:
        order.append(out)
        if nxt < n_blk:
            conv_and_dot()
    while nxt < n_blk:
        conv_and_dot()
    order.append(b_last)
    n_units = 1 + 2 * n_blk + len(a_z) + 2 + len(b_scans) + len(b_gates) + len(b_outs)
    assert len(order) == n_units and len(set(map(id, order))) == n_units
    return order


def _ssd_kernel(*refs, sb_n, tt, nt, n_tiles, has_state):
    if has_state:
        (x_ref, cprev_ref, h0_ref, *refs) = refs
    else:
        (x_ref, *refs) = refs
    (wz_ref, wdt_ref, cw_ref, cb_ref, dtb_ref, alog_ref, dexp_ref, nw_ref, wout_ref, e2_ref,
     g_ref, b_ref, o_ref, cnew_ref, hout_ref,
     ext_s, dtx_s, acx_s, y_s, yb_s, ac_s, h_s, *handoffs) = refs
    n_hand = len(handoffs) // 2
    hand = (tuple(handoffs[:n_hand]), tuple(handoffs[n_hand:]))
    i = pl.program_id(0)
    t_a = jnp.minimum(i, n_tiles - 1) % nt
    t_b = jnp.maximum(i - 1, 0) % nt
    hist = CONV_PAD - (SSD_CONV - 1)

    @pl.when(i == 0)
    def _zero_first_handoff():
        for ref in hand[1]:
            ref[...] = jnp.zeros(ref.shape, ref.dtype)

    @pl.when(t_a == 0)
    def _conv_history():
        ext_s[:, 0:CONV_PAD, :] = jnp.zeros((sb_n, CONV_PAD, SSD_CONV_DIM), F32)
        if has_state:
            for sb in range(sb_n):
                ext_s[sb, hist:CONV_PAD, :] = cprev_ref[sb]

    @pl.when(t_b == 0)
    def _state_in():
        if has_state:
            for sb in range(sb_n):
                for g in range(SSD_GROUPS):
                    hg = h0_ref[sb, g * HEADS_PER_GROUP:(g + 1) * HEADS_PER_GROUP]
                    h_s[sb, g] = hg.reshape(GROUP_W, SSD_STATE).T
        else:
            h_s[...] = jnp.zeros(h_s.shape, F32)

    def both_stages(write_to, read_from):
        stage_a = _ssd_stage_a(x_ref, wz_ref, wdt_ref, cw_ref, cb_ref, ext_s, write_to, sb_n=sb_n, tt=tt)
        stage_b = _ssd_stage_b(read_from, dtb_ref, alog_ref, dexp_ref, nw_ref, wout_ref, e2_ref, g_ref, b_ref,
                               o_ref, dtx_s, acx_s, y_s, yb_s, ac_s, h_s, sb_n=sb_n, tt=tt)
        for unit in _ssd_trace_order(stage_a, stage_b):
            unit()

    @pl.when(i % 2 == 0)
    def _even():
        both_stages(hand[0], hand[1])

    @pl.when(i % 2 == 1)
    def _odd():
        both_stages(hand[1], hand[0])

    @pl.when((t_a == nt - 1) & (i < n_tiles))
    def _conv_out():
        cnew_ref[...] = ext_s[:, hist:CONV_PAD, :]

    @pl.when((t_b == nt - 1) & (i >= 1))
    def _state_out():
        for sb in range(sb_n):
            for g in range(SSD_GROUPS):
                hout_ref[sb, g * HEADS_PER_GROUP:(g + 1) * HEADS_PER_GROUP] = (
                    h_s[sb, g].T.reshape(HEADS_PER_GROUP, SSD_HEAD_DIM, SSD_STATE))


def _ssd_layer(x, conv_prev, h0, w, *, sb_n, tt):
    nseq, seq_len, _ = x.shape
    has_state = h0 is not None
    rows = sb_n * tt
    nt = seq_len // tt
    n_tiles = (nseq // sb_n) * nt

    def tile_a(i):
        return jnp.minimum(i, n_tiles - 1)

    def tile_b(i):
        return jnp.maximum(i - 1, 0)

    x_spec = pl.BlockSpec((sb_n, tt, D_MODEL), lambda i: (tile_a(i) // nt, tile_a(i) % nt, 0))
    o_spec = pl.BlockSpec((sb_n, tt, D_MODEL), lambda i: (tile_b(i) // nt, tile_b(i) % nt, 0))
    conv_spec = pl.BlockSpec((sb_n, SSD_CONV - 1, SSD_CONV_DIM), lambda i: (tile_a(i) // nt, 0, 0))
    state_spec = pl.BlockSpec((sb_n, SSD_HEADS, SSD_HEAD_DIM, SSD_STATE), lambda i: (tile_b(i) // nt, 0, 0, 0))
    consts = (w["w_zx"], w["w_dt"], w["conv_w"], w["conv_b"], w["dt_bias"], w["a_log"], w["d_exp"],
              w["norm_w"], w["w_out"], w["e2"], w["ln_g"], w["ln_b"])
    in_specs = [x_spec] + ([conv_spec, state_spec] if has_state else []) + [_const_spec(c.shape) for c in consts]
    args = (x,) + ((conv_prev, h0) if has_state else ()) + consts
    handoff = [
        pltpu.VMEM((rows, SSD_D_INNER), F32),
        pltpu.VMEM((rows, SSD_D_INNER), F32),
        pltpu.VMEM((rows, 2 * SSD_BC), BF16),
        pltpu.VMEM((rows, LANES), F32),
        pltpu.VMEM((rows, D_MODEL), F32),
    ]
    scratch = [
        pltpu.VMEM((sb_n, tt + CONV_PAD, SSD_CONV_DIM), F32),
        pltpu.VMEM((rows, SSD_D_INNER), F32),
        pltpu.VMEM((rows, SSD_D_INNER), F32),
        pltpu.VMEM((rows, SSD_D_INNER), F32),
        pltpu.VMEM((rows, SSD_D_INNER), BF16),
        pltpu.VMEM((rows, LANES), F32),
        pltpu.VMEM((sb_n, SSD_GROUPS, SSD_STATE, GROUP_W), F32),
    ] + handoff + handoff
    out_shape = (
        jax.ShapeDtypeStruct(x.shape, F32),
        jax.ShapeDtypeStruct((nseq, SSD_CONV - 1, SSD_CONV_DIM), F32),
        jax.ShapeDtypeStruct((nseq, SSD_HEADS, SSD_HEAD_DIM, SSD_STATE), F32),
    )
    return pl.pallas_call(
        functools.partial(_ssd_kernel, sb_n=sb_n, tt=tt, nt=nt, n_tiles=n_tiles, has_state=has_state),
        grid=(n_tiles + 1,),
        in_specs=in_specs,
        out_specs=(o_spec, conv_spec, state_spec),
        out_shape=out_shape,
        scratch_shapes=scratch,
        compiler_params=pltpu.CompilerParams(dimension_semantics=("arbitrary",), vmem_limit_bytes=VMEM_LIMIT),
        name="ssd_state" if has_state else "ssd_fresh",
    )(*args)


def _ffn_kernel(x_ref, w1_ref, w2_ref, g_ref, b_ref, o_ref):
    x = x_ref[...]
    h = jnp.dot(x.astype(BF16), w1_ref[...], preferred_element_type=F32)
    h = jnp.maximum(h, 0.0)
    y = jnp.dot((h * h).astype(BF16), w2_ref[...], preferred_element_type=F32)
    o_ref[...] = _layer_norm(DN_ALPHA * x + y, g_ref[...], b_ref[...])


def _ffn_layer(x2d, w1, w2, g, b, *, tm):
    n = x2d.shape[0]
    row_spec = pl.BlockSpec((tm, D_MODEL), lambda i: (i, 0))
    return pl.pallas_call(
        _ffn_kernel,
        grid=(n // tm,),
        in_specs=[row_spec, _const_spec(w1.shape), _const_spec(w2.shape), _const_spec(g.shape), _const_spec(b.shape)],
        out_specs=row_spec,
        out_shape=jax.ShapeDtypeStruct(x2d.shape, F32),
        compiler_params=pltpu.CompilerParams(dimension_semantics=("arbitrary",), vmem_limit_bytes=VMEM_LIMIT),
        name="ffn",
    )(x2d, w1, w2, g, b)


def _cmlp_kernel(x_ref, win_ref, bin_ref, lng_ref, lnb_ref, ws_ref, bs_ref, wout_ref, g_ref, b_ref,
                 o_ref, *rest, blk, emit_v):
    if emit_v:
        v_ref, u_s, vpre_s, v_s = rest
    else:
        u_s, vpre_s, v_s = rest
    x = x_ref[...]
    tm = x.shape[0]
    xb = x.astype(BF16)

    def gelu_block(c0):
        cs = slice(c0, c0 + COL_BLK_CM)
        h = jnp.dot(xb, win_ref[:, cs], preferred_element_type=F32) + bin_ref[:, cs]
        return 0.5 * h * (1.0 + jnp.tanh(0.7978845608028654 * (h + 0.044715 * (h * h * h))))

    for c0 in range(0, CM_WIDTH, COL_BLK_CM):
        vpre_s[:, c0:c0 + COL_BLK_CM] = gelu_block(CM_WIDTH + c0)
    v = _layer_norm(vpre_s[...], lng_ref[...], lnb_ref[...])
    if emit_v:
        v_ref[...] = v
    v_s[...] = v.astype(BF16)
    for c0 in range(0, CM_WIDTH, COL_BLK_CM):
        u_s[:, c0:c0 + COL_BLK_CM] = gelu_block(c0)
    ri = lax.broadcasted_iota(jnp.int32, (blk, blk), 0)
    ci = lax.broadcasted_iota(jnp.int32, (blk, blk), 1)
    for g in range(CM_GROUPS):
        gs = slice(g * CM_GROUP_W, (g + 1) * CM_GROUP_W)
        ws_g = jnp.where(ci <= ri, ws_ref[g, 0:blk, 0:blk], 0.0).astype(BF16)
        for i in range(tm // blk):
            rs = slice(i * blk, (i + 1) * blk)
            s = jnp.dot(ws_g, v_s[rs, gs], preferred_element_type=F32) + bs_ref[0:blk, gs]
            u_s[rs, gs] = u_s[rs, gs] * s
    mix = jnp.dot(u_s[...].astype(BF16), wout_ref[...], preferred_element_type=F32)
    o_ref[...] = _layer_norm(DN_ALPHA * x + mix, g_ref[...], b_ref[...])


def _cmlp_layer(x2d, w, *, tm, blk, emit_v):
    n = x2d.shape[0]
    row_spec = pl.BlockSpec((tm, D_MODEL), lambda i: (i, 0))
    consts = (w["w_in"], w["b_in"], w["ln_g"], w["ln_b"], w["w_s"], w["b_s"], w["w_out"], w["ln1_g"], w["ln1_b"])
    out_shape = [jax.ShapeDtypeStruct(x2d.shape, F32)]
    out_specs = [row_spec]
    if emit_v:
        out_shape.append(jax.ShapeDtypeStruct((n, CM_WIDTH), F32))
        out_specs.append(pl.BlockSpec((tm, CM_WIDTH), lambda i: (i, 0)))
    return pl.pallas_call(
        functools.partial(_cmlp_kernel, blk=blk, emit_v=emit_v),
        grid=(n // tm,),
        in_specs=[row_spec] + [_const_spec(c.shape) for c in consts],
        out_specs=out_specs,
        out_shape=out_shape,
        scratch_shapes=[pltpu.VMEM((tm, CM_WIDTH), F32), pltpu.VMEM((tm, CM_WIDTH), F32),
                        pltpu.VMEM((tm, CM_WIDTH), BF16)],
        compiler_params=pltpu.CompilerParams(dimension_semantics=("arbitrary",), vmem_limit_bytes=VMEM_LIMIT),
        name="cmlp_v" if emit_v else "cmlp",
    )(x2d, *consts)


def kernel(x_prompt, x_sample, state_ssm, state_conv, ssd_w_in, ssd_conv_w, ssd_conv_b, ssd_dt_bias, ssd_a_log, ssd_d, ssd_norm_w, ssd_w_out, cm_w_in, cm_b_in, cm_ln_g, cm_ln_b, cm_w_s, cm_b_s, cm_w_out, ffn_w1, ffn_w2, ln1_g, ln1_b, ln2_g, ln2_b):
    row = lambda a: a.reshape(1, -1)
    perm = jnp.asarray(HEAD_PERM)

    def head_lanes(a):
        return jnp.pad(a[perm], (0, LANES - SSD_HEADS)).reshape(1, LANES)

    w_in = ssd_w_in[0]
    zx_end = SSD_D_INNER + SSD_CONV_DIM
    ssd_w = {
        "w_zx": w_in[:, 0:zx_end].astype(BF16),
        "w_dt": jnp.pad(w_in[:, zx_end:][:, perm], ((0, 0), (0, LANES - SSD_HEADS))).astype(BF16),
        "conv_w": ssd_conv_w[0],
        "conv_b": row(ssd_conv_b[0]),
        "dt_bias": head_lanes(ssd_dt_bias[0]),
        "a_log": head_lanes(ssd_a_log[0]),
        "d_exp": row(jnp.repeat(ssd_d[0], SSD_HEAD_DIM)),
        "norm_w": row(ssd_norm_w[0]),
        "w_out": ssd_w_out[0].astype(BF16),
        "e2": jnp.asarray(_expand_matrix(), BF16),
        "ln_g": row(ln1_g[0]),
        "ln_b": row(ln1_b[0]),
    }
    cm_w = {
        "w_in": cm_w_in[0].astype(BF16),
        "b_in": row(cm_b_in[0]),
        "ln_g": row(cm_ln_g[0]),
        "ln_b": row(cm_ln_b[0]),
        "w_s": cm_w_s[0],
        "b_s": jnp.repeat(cm_b_s[0].T, CM_GROUP_W, axis=1),
        "w_out": cm_w_out[0].astype(BF16),
        "ln1_g": row(ln1_g[1]),
        "ln1_b": row(ln1_b[1]),
    }
    ffn = [(ffn_w1[i].astype(BF16), ffn_w2[i].astype(BF16), row(ln2_g[i]), row(ln2_b[i])) for i in range(DEPTH)]

    def trunk(x, conv_prev, h0, *, sb_n, tt, blk, emit_v):
        shape = x.shape
        x1, conv_new, h_new = _ssd_layer(x, conv_prev, h0, ssd_w, sb_n=sb_n, tt=tt)
        x2 = _ffn_layer(x1.reshape(-1, D_MODEL), *ffn[0], tm=512)
        res = _cmlp_layer(x2, cm_w, tm=512, blk=blk, emit_v=emit_v)
        x4 = _ffn_layer(res[0], *ffn[1], tm=512)
        v_rows = res[1].reshape(shape[0], shape[1], CM_WIDTH) if emit_v else None
        return x4.reshape(shape), conv_new, h_new, v_rows

    y_p, conv_p, h_p, _ = trunk(x_prompt, None, None, sb_n=1, tt=256, blk=CM_BLOCK, emit_v=False)
    y_s, conv_s, h_s, v_s = trunk(x_sample, state_conv[0], state_ssm[0], sb_n=2, tt=CHUNK,
                                  blk=x_sample.shape[1], emit_v=True)
    return (y_p, y_s, h_p[None], conv_p[None], h_s[None], conv_s[None], v_s[None])
```

```python
import functools

import numpy as np
import jax
import jax.numpy as jnp
from jax import lax
from jax.experimental import pallas as pl
from jax.experimental.pallas import tpu as pltpu

F32 = jnp.float32
BF16 = jnp.bfloat16

D_MODEL = 1024
DEPTH = 2
CHUNK = 64
SSD_D_INNER = 2 * D_MODEL
SSD_HEAD_DIM = 64
SSD_HEADS = SSD_D_INNER // SSD_HEAD_DIM
SSD_GROUPS = 8
SSD_STATE = 128
SSD_CONV = 4
SSD_BC = SSD_GROUPS * SSD_STATE
SSD_CONV_DIM = SSD_D_INNER + 2 * SSD_BC
HEADS_PER_GROUP = SSD_HEADS // SSD_GROUPS
GROUP_W = HEADS_PER_GROUP * SSD_HEAD_DIM
NORM_GROUP_W = SSD_D_INNER // SSD_GROUPS
CM_BLOCK = 128
CM_WIDTH = 2 * D_MODEL
CM_GROUPS = 8
CM_GROUP_W = CM_WIDTH // CM_GROUPS
FFN_HIDDEN = 4 * D_MODEL
DN_ALPHA = (2 * DEPTH) ** 0.25
LN_EPS = 1e-5

LANES = 128
CONV_PAD = 8
COL_BLK = 256
OUT_BLK = 256
COL_BLK_CM = 512
SCAN_UNIT_GROUPS = 4
VMEM_LIMIT = 56 * 1024 * 1024

HEAD_PERM = np.concatenate([np.arange(0, SSD_HEADS, 2), np.arange(1, SSD_HEADS, 2)])


def _expand_matrix():
    e = np.zeros((2 * LANES, SSD_D_INNER), np.float32)
    for k, h in enumerate(HEAD_PERM):
        e[k, h * SSD_HEAD_DIM:(h + 1) * SSD_HEAD_DIM] = 1.0
        e[LANES + k, h * SSD_HEAD_DIM:(h + 1) * SSD_HEAD_DIM] = 1.0
    return e


def _layer_norm(x, g, b):
    mu = jnp.mean(x, axis=-1, keepdims=True)
    xc = x - mu
    var = jnp.mean(xc * xc, axis=-1, keepdims=True)
    return xc * lax.rsqrt(var + LN_EPS) * g + b


def _sigmoid(x):
    return 1.0 / (1.0 + jnp.exp(-x))


def _split2(v):
    hi = v.astype(BF16)
    lo = (v - hi.astype(F32)).astype(BF16)
    return jnp.concatenate([hi, lo], axis=1)


def _const_spec(shape):
    nd = len(shape)
    return pl.BlockSpec(shape, lambda *_: (0,) * nd, pipeline_mode=pl.Buffered(1))


def _ssd_stage_a(x_ref, wz_ref, wdt_ref, cw_ref, cb_ref, ext_s, handoff, *, sb_n, tt):
    z_w, xs_w, bc_w, dtr_w = handoff
    rows = sb_n * tt
    hist = CONV_PAD - (SSD_CONV - 1)
    live = {}

    def first():
        live["xb"] = x_ref[...].reshape(rows, D_MODEL).astype(BF16)
        dtr_w[...] = jnp.dot(live["xb"], wdt_ref[...], preferred_element_type=F32)

    def dot_unit(c0):
        cs = slice(c0, c0 + COL_BLK)
        raw = jnp.dot(live["xb"], wz_ref[:, SSD_D_INNER + c0:SSD_D_INNER + c0 + COL_BLK],
                      preferred_element_type=F32)
        for sb in range(sb_n):
            ext_s[sb, CONV_PAD:CONV_PAD + tt, cs] = raw[sb * tt:(sb + 1) * tt]

    def conv_unit(c0):
        cs = slice(c0, c0 + COL_BLK)
        for sb in range(sb_n):
            ext = ext_s[sb, :, cs]
            acc = cb_ref[:, cs] + ext[CONV_PAD:] * cw_ref[SSD_CONV - 1:SSD_CONV, cs]
            for k in range(SSD_CONV - 1):
                back = SSD_CONV - 1 - k
                acc = acc + pltpu.roll(ext, back, axis=0)[CONV_PAD:] * cw_ref[k:k + 1, cs]
            act = acc * _sigmoid(acc)
            rs = slice(sb * tt, (sb + 1) * tt)
            if c0 < SSD_D_INNER:
                xs_w[rs, cs] = act
            else:
                bc_w[rs, c0 - SSD_D_INNER:c0 - SSD_D_INNER + COL_BLK] = act.astype(BF16)
            ext_s[sb, hist:CONV_PAD, cs] = ext[tt + hist:tt + CONV_PAD]

    def z_unit(c0):
        z_w[:, c0:c0 + COL_BLK] = jnp.dot(live["xb"], wz_ref[:, c0:c0 + COL_BLK], preferred_element_type=F32)

    blocks = range(0, SSD_CONV_DIM, COL_BLK)
    return (first,
            [functools.partial(dot_unit, c0) for c0 in blocks],
            [functools.partial(conv_unit, c0) for c0 in blocks],
            [functools.partial(z_unit, c0) for c0 in range(0, SSD_D_INNER, COL_BLK)])


def _ssd_stage_b(handoff, dtb_ref, alog_ref, dexp_ref, nw_ref, e2_ref, yb_w,
                 dtx_s, acx_s, y_s, ac_s, h_s, *, sb_n, tt):
    z_r, xs_r, bc_r, dtr_r = handoff
    rows = sb_n * tt
    nch = tt // CHUNK
    n_pairs = SSD_HEADS // 2
    live = {}

    def prologue():
        v = dtr_r[...] + dtb_ref[...]
        dt = jnp.maximum(v, 0.0) + jnp.log1p(jnp.exp(-jnp.abs(v)))
        a_neg = -jnp.exp(alog_ref[...])
        d_a = dt * a_neg
        ri = lax.broadcasted_iota(jnp.int32, (rows, rows), 0)
        ci = lax.broadcasted_iota(jnp.int32, (rows, rows), 1)
        same_chunk = (ri & -CHUNK) == (ci & -CHUNK)
        tril = jnp.where((ci <= ri) & same_chunk, 1.0, 0.0).astype(BF16)
        cs = jnp.dot(tril, _split2(d_a), preferred_element_type=F32)
        acum = cs[:, 0:LANES] + cs[:, LANES:]
        ac_s[...] = acum
        acx_s[...] = jnp.dot(_split2(acum), e2_ref[...], preferred_element_type=F32)
        dtx_s[...] = xs_r[...] * jnp.dot(_split2(dt), e2_ref[...], preferred_element_type=F32)
        row_i = lax.broadcasted_iota(jnp.int32, (CHUNK, LANES), 0)
        lane_i = lax.broadcasted_iota(jnp.int32, (CHUNK, LANES), 1)
        live["causal2"] = (lane_i & (CHUNK - 1)) <= row_i
        live["first_half"] = lane_i < CHUNK

    def scan_unit(chunk, g0):
        sb = chunk // nch
        r0 = chunk * CHUNK
        rs = slice(r0, r0 + CHUNK)
        if g0 == 0:
            ac_t = ac_s[rs, :].T
            live["pair_rows"] = jnp.concatenate([ac_t[0:n_pairs], ac_t[n_pairs:2 * n_pairs]], axis=1)
        pair_rows, causal2, first_half = live["pair_rows"], live["causal2"], live["first_half"]
        groups = range(g0, g0 + SCAN_UNIT_GROUPS)
        b, c, h, acx, cb2, y_off = {}, {}, {}, {}, {}, {}
        for g in groups:
            gs = slice(g * GROUP_W, (g + 1) * GROUP_W)
            b[g] = bc_r[rs, g * SSD_STATE:(g + 1) * SSD_STATE]
            c[g] = bc_r[rs, SSD_BC + g * SSD_STATE:SSD_BC + (g + 1) * SSD_STATE]
            bb = jnp.concatenate([b[g], b[g]], axis=0)
            cb2[g] = lax.dot_general(c[g], bb, (((1,), (1,)), ((), ())), preferred_element_type=F32)
            h[g] = h_s[sb, g]
            acx[g] = acx_s[rs, gs]
            y_off[g] = jnp.dot(c[g], h[g].astype(BF16), preferred_element_type=F32)
        y_diag = {}
        for g in groups:
            for jj in range(2):
                j = 2 * g + jj
                ls = slice(j * LANES, (j + 1) * LANES)
                seg = acx_s[rs, ls] - pair_rows[j:j + 1, :]
                decay = jnp.where(causal2, jnp.exp(seg), 0.0)
                m2 = (cb2[g] * decay).astype(BF16)
                xp = dtx_s[rs, ls]
                xbd = jnp.concatenate([jnp.where(first_half, xp, 0.0), jnp.where(first_half, 0.0, xp)], axis=0)
                y_diag[j] = jnp.dot(m2, xbd.astype(BF16), preferred_element_type=F32)
        upd, a_last = {}, {}
        for g in groups:
            gs = slice(g * GROUP_W, (g + 1) * GROUP_W)
            a_last[g] = acx_s[r0 + CHUNK - 1:r0 + CHUNK, gs]
            xw = (dtx_s[rs, gs] * jnp.exp(a_last[g] - acx[g])).astype(BF16)
            upd[g] = lax.dot_general(b[g], xw, (((0,), (0,)), ((), ())), preferred_element_type=F32)
        for g in groups:
            y_g = jnp.concatenate([y_diag[2 * g], y_diag[2 * g + 1]], axis=1) + y_off[g] * jnp.exp(acx[g])
            y_s[rs, g * GROUP_W:(g + 1) * GROUP_W] = y_g
            h_s[sb, g] = h[g] * jnp.exp(a_last[g]) + upd[g]

    def gate_unit(g):
        gs = slice(g * NORM_GROUP_W, (g + 1) * NORM_GROUP_W)
        y = y_s[:, gs] + dexp_ref[:, gs] * xs_r[:, gs]
        zg = z_r[:, gs]
        gg = y * (zg * _sigmoid(zg))
        ms = jnp.mean(gg * gg, axis=-1, keepdims=True)
        yb_w[:, gs] = (gg * lax.rsqrt(ms + LN_EPS) * nw_ref[:, gs]).astype(BF16)

    return (prologue,
            [functools.partial(scan_unit, c, g) for c in range(sb_n * nch)
             for g in range(0, SSD_GROUPS, SCAN_UNIT_GROUPS)],
            [functools.partial(gate_unit, g) for g in range(SSD_GROUPS)])


def _ssd_stage_c(yb_r, x_ref, wout_ref, g_ref, b_ref, o_ref, *, sb_n, tt):
    rows = sb_n * tt

    def out_unit(c0):
        mix = jnp.dot(yb_r[...], wout_ref[:, c0:c0 + OUT_BLK], preferred_element_type=F32)
        o_ref[:, :, c0:c0 + OUT_BLK] = mix.reshape(sb_n, tt, OUT_BLK)

    def last():
        mix = o_ref[...].reshape(rows, D_MODEL)
        x = x_ref[...].reshape(rows, D_MODEL)
        o_ref[...] = _layer_norm(DN_ALPHA * x + mix, g_ref[...], b_ref[...]).reshape(sb_n, tt, D_MODEL)

    return [functools.partial(out_unit, c0) for c0 in range(0, D_MODEL, OUT_BLK)], last


def _ssd_trace_order(stage_a, stage_b, stage_c):
    a_first, a_dots, a_convs, a_z = stage_a
    b_pro, b_scans, b_gates = stage_b
    c_units = list(stage_c[0]) + [stage_c[1]]
    lead = 4
    n_blk = len(a_dots)
    order = [a_first, a_dots[0], a_dots[1], b_pro, a_dots[2], a_dots[3]]
    nxt_conv, nxt_dot = 0, lead

    def conv_and_dot():
        nonlocal nxt_conv, nxt_dot
        order.append(a_convs[nxt_conv])
        nxt_conv += 1
        if nxt_dot < n_blk:
            order.append(a_dots[nxt_dot])
            nxt_dot += 1

    n_scan = len(b_scans)
    n_beside_scans = 10 * n_scan * SCAN_UNIT_GROUPS // 32
    n_c = len(c_units)
    for u, scan in enumerate(b_scans):
        order.append(scan)
        for _ in range((u + 1) * n_beside_scans // n_scan - u * n_beside_scans // n_scan):
            conv_and_dot()
        for _ in range((u + 1) * n_c // n_scan - u * n_c // n_scan):
            order.append(c_units.pop(0))
    for g, gate in enumerate(b_gates):
        order.append(gate)
        order.extend(a_z[g:g + 1])
        if nxt_conv < n_blk:
            conv_and_dot()
    order.extend(a_z[len(b_gates):])
    while nxt_conv < n_blk:
        conv_and_dot()
    order.extend(c_units)
    n_units = 1 + 2 * n_blk + len(a_z) + 1 + len(b_scans) + len(b_gates) + len(stage_c[0]) + 1
    assert len(order) == n_units and len(set(map(id, order))) == n_units
    return order


def _ssd_kernel(*refs, sb_n, tt, nt, n_tiles, has_state):
    if has_state:
        (x_ref, xres_ref, cprev_ref, h0_ref, *refs) = refs
    else:
        (x_ref, xres_ref, *refs) = refs
    (wz_ref, wdt_ref, cw_ref, cb_ref, dtb_ref, alog_ref, dexp_ref, nw_ref, wout_ref, e2_ref,
     g_ref, b_ref, o_ref, cnew_ref, hout_ref,
     ext_s, dtx_s, acx_s, y_s, ac_s, h_s, yb0_s, yb1_s, *handoffs) = refs
    n_hand = len(handoffs) // 2
    hand = (tuple(handoffs[:n_hand]), tuple(handoffs[n_hand:]))
    yb = (yb0_s, yb1_s)
    i = pl.program_id(0)
    t_a = jnp.minimum(i, n_tiles - 1) % nt
    t_b = jnp.clip(i - 1, 0, n_tiles - 1) % nt
    hist = CONV_PAD - (SSD_CONV - 1)

    @pl.when(i == 0)
    def _zero_first_handoff():
        for ref in hand[1] + (yb[0],):
            ref[...] = jnp.zeros(ref.shape, ref.dtype)

    @pl.when(t_a == 0)
    def _conv_history():
        ext_s[:, 0:CONV_PAD, :] = jnp.zeros((sb_n, CONV_PAD, SSD_CONV_DIM), F32)
        if has_state:
            for sb in range(sb_n):
                ext_s[sb, hist:CONV_PAD, :] = cprev_ref[sb]

    @pl.when(t_b == 0)
    def _state_in():
        if has_state:
            for sb in range(sb_n):
                for g in range(SSD_GROUPS):
                    hg = h0_ref[sb, g * HEADS_PER_GROUP:(g + 1) * HEADS_PER_GROUP]
                    h_s[sb, g] = hg.reshape(GROUP_W, SSD_STATE).T
        else:
            h_s[...] = jnp.zeros(h_s.shape, F32)

    def all_stages(parity):
        stage_a = _ssd_stage_a(x_ref, wz_ref, wdt_ref, cw_ref, cb_ref, ext_s, hand[parity], sb_n=sb_n, tt=tt)
        stage_b = _ssd_stage_b(hand[1 - parity], dtb_ref, alog_ref, dexp_ref, nw_ref, e2_ref, yb[1 - parity],
                               dtx_s, acx_s, y_s, ac_s, h_s, sb_n=sb_n, tt=tt)
        stage_c = _ssd_stage_c(yb[parity], xres_ref, wout_ref, g_ref, b_ref, o_ref, sb_n=sb_n, tt=tt)
        for unit in _ssd_trace_order(stage_a, stage_b, stage_c):
            unit()

    @pl.when(i % 2 == 0)
    def _even():
        all_stages(0)

    @pl.when(i % 2 == 1)
    def _odd():
        all_stages(1)

    @pl.when((t_a == nt - 1) & (i < n_tiles))
    def _conv_out():
        cnew_ref[...] = ext_s[:, hist:CONV_PAD, :]

    @pl.when((t_b == nt - 1) & (i >= 1) & (i <= n_tiles))
    def _state_out():
        for sb in range(sb_n):
            for g in range(SSD_GROUPS):
                hout_ref[sb, g * HEADS_PER_GROUP:(g + 1) * HEADS_PER_GROUP] = (
                    h_s[sb, g].T.reshape(HEADS_PER_GROUP, SSD_HEAD_DIM, SSD_STATE))


def _ssd_layer(x, conv_prev, h0, w, *, sb_n, tt):
    nseq, seq_len, _ = x.shape
    has_state = h0 is not None
    rows = sb_n * tt
    nt = seq_len // tt
    n_tiles = (nseq // sb_n) * nt

    def tile_a(i):
        return jnp.minimum(i, n_tiles - 1)

    def tile_b(i):
        return jnp.clip(i - 1, 0, n_tiles - 1)

    def tile_c(i):
        return jnp.maximum(i - 2, 0)

    x_spec = pl.BlockSpec((sb_n, tt, D_MODEL), lambda i: (tile_a(i) // nt, tile_a(i) % nt, 0))
    o_spec = pl.BlockSpec((sb_n, tt, D_MODEL), lambda i: (tile_c(i) // nt, tile_c(i) % nt, 0))
    conv_spec = pl.BlockSpec((sb_n, SSD_CONV - 1, SSD_CONV_DIM), lambda i: (tile_a(i) // nt, 0, 0))
    state_spec = pl.BlockSpec((sb_n, SSD_HEADS, SSD_HEAD_DIM, SSD_STATE), lambda i: (tile_b(i) // nt, 0, 0, 0))
    consts = (w["w_zx"], w["w_dt"], w["conv_w"], w["conv_b"], w["dt_bias"], w["a_log"], w["d_exp"],
              w["norm_w"], w["w_out"], w["e2"], w["ln_g"], w["ln_b"])
    in_specs = ([x_spec, o_spec] + ([conv_spec, state_spec] if has_state else [])
                + [_const_spec(c.shape) for c in consts])
    args = (x, x) + ((conv_prev, h0) if has_state else ()) + consts
    handoff = [
        pltpu.VMEM((rows, SSD_D_INNER), F32),
        pltpu.VMEM((rows, SSD_D_INNER), F32),
        pltpu.VMEM((rows, 2 * SSD_BC), BF16),
        pltpu.VMEM((rows, LANES), F32),
    ]
    scratch = [
        pltpu.VMEM((sb_n, tt + CONV_PAD, SSD_CONV_DIM), F32),
        pltpu.VMEM((rows, SSD_D_INNER), F32),
        pltpu.VMEM((rows, SSD_D_INNER), F32),
        pltpu.VMEM((rows, SSD_D_INNER), F32),
        pltpu.VMEM((rows, LANES), F32),
        pltpu.VMEM((sb_n, SSD_GROUPS, SSD_STATE, GROUP_W), F32),
        pltpu.VMEM((rows, SSD_D_INNER), BF16),
        pltpu.VMEM((rows, SSD_D_INNER), BF16),
    ] + handoff + handoff
    out_shape = (
        jax.ShapeDtypeStruct(x.shape, F32),
        jax.ShapeDtypeStruct((nseq, SSD_CONV - 1, SSD_CONV_DIM), F32),
        jax.ShapeDtypeStruct((nseq, SSD_HEADS, SSD_HEAD_DIM, SSD_STATE), F32),
    )
    return pl.pallas_call(
        functools.partial(_ssd_kernel, sb_n=sb_n, tt=tt, nt=nt, n_tiles=n_tiles, has_state=has_state),
        grid=(n_tiles + 2,),
        in_specs=in_specs,
        out_specs=(o_spec, conv_spec, state_spec),
        out_shape=out_shape,
        scratch_shapes=scratch,
        compiler_params=pltpu.CompilerParams(dimension_semantics=("arbitrary",), vmem_limit_bytes=VMEM_LIMIT),
        name="ssd_state" if has_state else "ssd_fresh",
    )(*args)


def _ffn_kernel(x_ref, w1_ref, w2_ref, g_ref, b_ref, o_ref):
    x = x_ref[...]
    h = jnp.dot(x.astype(BF16), w1_ref[...], preferred_element_type=F32)
    h = jnp.maximum(h, 0.0)
    y = jnp.dot((h * h).astype(BF16), w2_ref[...], preferred_element_type=F32)
    o_ref[...] = _layer_norm(DN_ALPHA * x + y, g_ref[...], b_ref[...])


def _ffn_layer(x2d, w1, w2, g, b, *, tm):
    n = x2d.shape[0]
    row_spec = pl.BlockSpec((tm, D_MODEL), lambda i: (i, 0))
    return pl.pallas_call(
        _ffn_kernel,
        grid=(n // tm,),
        in_specs=[row_spec, _const_spec(w1.shape), _const_spec(w2.shape), _const_spec(g.shape), _const_spec(b.shape)],
        out_specs=row_spec,
        out_shape=jax.ShapeDtypeStruct(x2d.shape, F32),
        compiler_params=pltpu.CompilerParams(dimension_semantics=("arbitrary",), vmem_limit_bytes=VMEM_LIMIT),
        name="ffn",
    )(x2d, w1, w2, g, b)


def _cmlp_kernel(x_ref, win_ref, bin_ref, lng_ref, lnb_ref, ws_ref, bs_ref, wout_ref, g_ref, b_ref,
                 o_ref, *rest, blk, emit_v):
    if emit_v:
        v_ref, u_s, vpre_s, v_s = rest
    else:
        u_s, vpre_s, v_s = rest
    x = x_ref[...]
    tm = x.shape[0]
    xb = x.astype(BF16)

    def gelu_block(c0):
        cs = slice(c0, c0 + COL_BLK_CM)
        h = jnp.dot(xb, win_ref[:, cs], preferred_element_type=F32) + bin_ref[:, cs]
        return 0.5 * h * (1.0 + jnp.tanh(0.7978845608028654 * (h + 0.044715 * (h * h * h))))

    for c0 in range(0, CM_WIDTH, COL_BLK_CM):
        vpre_s[:, c0:c0 + COL_BLK_CM] = gelu_block(CM_WIDTH + c0)
    v = _layer_norm(vpre_s[...], lng_ref[...], lnb_ref[...])
    if emit_v:
        v_ref[...] = v
    v_s[...] = v.astype(BF16)
    for c0 in range(0, CM_WIDTH, COL_BLK_CM):
        u_s[:, c0:c0 + COL_BLK_CM] = gelu_block(c0)
    ri = lax.broadcasted_iota(jnp.int32, (blk, blk), 0)
    ci = lax.broadcasted_iota(jnp.int32, (blk, blk), 1)
    for g in range(CM_GROUPS):
        gs = slice(g * CM_GROUP_W, (g + 1) * CM_GROUP_W)
        ws_g = jnp.where(ci <= ri, ws_ref[g, 0:blk, 0:blk], 0.0).astype(BF16)
        for i in range(tm // blk):
            rs = slice(i * blk, (i + 1) * blk)
            s = jnp.dot(ws_g, v_s[rs, gs], preferred_element_type=F32) + bs_ref[0:blk, gs]
            u_s[rs, gs] = u_s[rs, gs] * s
    mix = jnp.dot(u_s[...].astype(BF16), wout_ref[...], preferred_element_type=F32)
    o_ref[...] = _layer_norm(DN_ALPHA * x + mix, g_ref[...], b_ref[...])


def _cmlp_layer(x2d, w, *, tm, blk, emit_v):
    n = x2d.shape[0]
    row_spec = pl.BlockSpec((tm, D_MODEL), lambda i: (i, 0))
    consts = (w["w_in"], w["b_in"], w["ln_g"], w["ln_b"], w["w_s"], w["b_s"], w["w_out"], w["ln1_g"], w["ln1_b"])
    out_shape = [jax.ShapeDtypeStruct(x2d.shape, F32)]
    out_specs = [row_spec]
    if emit_v:
        out_shape.append(jax.ShapeDtypeStruct((n, CM_WIDTH), F32))
        out_specs.append(pl.BlockSpec((tm, CM_WIDTH), lambda i: (i, 0)))
    return pl.pallas_call(
        functools.partial(_cmlp_kernel, blk=blk, emit_v=emit_v),
        grid=(n // tm,),
        in_specs=[row_spec] + [_const_spec(c.shape) for c in consts],
        out_specs=out_specs,
        out_shape=out_shape,
        scratch_shapes=[pltpu.VMEM((tm, CM_WIDTH), F32), pltpu.VMEM((tm, CM_WIDTH), F32),
                        pltpu.VMEM((tm, CM_WIDTH), BF16)],
        compiler_params=pltpu.CompilerParams(dimension_semantics=("arbitrary",), vmem_limit_bytes=VMEM_LIMIT),
        name="cmlp_v" if emit_v else "cmlp",
    )(x2d, *consts)


def kernel(x_prompt, x_sample, state_ssm, state_conv, ssd_w_in, ssd_conv_w, ssd_conv_b, ssd_dt_bias, ssd_a_log, ssd_d, ssd_norm_w, ssd_w_out, cm_w_in, cm_b_in, cm_ln_g, cm_ln_b, cm_w_s, cm_b_s, cm_w_out, ffn_w1, ffn_w2, ln1_g, ln1_b, ln2_g, ln2_b):
    row = lambda a: a.reshape(1, -1)
    perm = jnp.asarray(HEAD_PERM)

    def head_lanes(a):
        return jnp.pad(a[perm], (0, LANES - SSD_HEADS)).reshape(1, LANES)

    w_in = ssd_w_in[0]
    zx_end = SSD_D_INNER + SSD_CONV_DIM
    ssd_w = {
        "w_zx": w_in[:, 0:zx_end].astype(BF16),
        "w_dt": jnp.pad(w_in[:, zx_end:][:, perm], ((0, 0), (0, LANES - SSD_HEADS))).astype(BF16),
        "conv_w": ssd_conv_w[0],
        "conv_b": row(ssd_conv_b[0]),
        "dt_bias": head_lanes(ssd_dt_bias[0]),
        "a_log": head_lanes(ssd_a_log[0]),
        "d_exp": row(jnp.repeat(ssd_d[0], SSD_HEAD_DIM)),
        "norm_w": row(ssd_norm_w[0]),
        "w_out": ssd_w_out[0].astype(BF16),
        "e2": jnp.asarray(_expand_matrix(), BF16),
        "ln_g": row(ln1_g[0]),
        "ln_b": row(ln1_b[0]),
    }
    cm_w = {
        "w_in": cm_w_in[0].astype(BF16),
        "b_in": row(cm_b_in[0]),
        "ln_g": row(cm_ln_g[0]),
        "ln_b": row(cm_ln_b[0]),
        "w_s": cm_w_s[0],
        "b_s": jnp.repeat(cm_b_s[0].T, CM_GROUP_W, axis=1),
        "w_out": cm_w_out[0].astype(BF16),
        "ln1_g": row(ln1_g[1]),
        "ln1_b": row(ln1_b[1]),
    }
    ffn = [(ffn_w1[i].astype(BF16), ffn_w2[i].astype(BF16), row(ln2_g[i]), row(ln2_b[i])) for i in range(DEPTH)]

    def trunk(x, conv_prev, h0, *, sb_n, tt, blk, emit_v):
        shape = x.shape
        x1, conv_new, h_new = _ssd_layer(x, conv_prev, h0, ssd_w, sb_n=sb_n, tt=tt)
        x2 = _ffn_layer(x1.reshape(-1, D_MODEL), *ffn[0], tm=512)
        res = _cmlp_layer(x2, cm_w, tm=512, blk=blk, emit_v=emit_v)
        x4 = _ffn_layer(res[0], *ffn[1], tm=512)
        v_rows = res[1].reshape(shape[0], shape[1], CM_WIDTH) if emit_v else None
        return x4.reshape(shape), conv_new, h_new, v_rows

    y_p, conv_p, h_p, _ = trunk(x_prompt, None, None, sb_n=1, tt=256, blk=CM_BLOCK, emit_v=False)
    y_s, conv_s, h_s, v_s = trunk(x_sample, state_conv[0], state_ssm[0], sb_n=2, tt=CHUNK,
                                  blk=x_sample.shape[1], emit_v=True)
    return (y_p, y_s, h_p[None], conv_p[None], h_s[None], conv_s[None], v_s[None])
```

```python
import functools

import numpy as np
import jax
import jax.numpy as jnp
from jax import lax
from jax.experimental import pallas as pl
from jax.experimental.pallas import tpu as pltpu

F32 = jnp.float32
BF16 = jnp.bfloat16

D_MODEL = 1024
DEPTH = 2
CHUNK = 64
SSD_D_INNER = 2 * D_MODEL
SSD_HEAD_DIM = 64
SSD_HEADS = SSD_D_INNER // SSD_HEAD_DIM
SSD_GROUPS = 8
SSD_STATE = 128
SSD_CONV = 4
SSD_BC = SSD_GROUPS * SSD_STATE
SSD_CONV_DIM = SSD_D_INNER + 2 * SSD_BC
HEADS_PER_GROUP = SSD_HEADS // SSD_GROUPS
GROUP_W = HEADS_PER_GROUP * SSD_HEAD_DIM
NORM_GROUP_W = SSD_D_INNER // SSD_GROUPS
CM_BLOCK = 128
CM_WIDTH = 2 * D_MODEL
CM_GROUPS = 8
CM_GROUP_W = CM_WIDTH // CM_GROUPS
FFN_HIDDEN = 4 * D_MODEL
DN_ALPHA = (2 * DEPTH) ** 0.25
LN_EPS = 1e-5

LANES = 128
CONV_PAD = 8
COL_BLK = 256
OUT_BLK = 256
COL_BLK_CM = 512
SCAN_UNIT_GROUPS = 4
VMEM_LIMIT = 56 * 1024 * 1024

HEAD_PERM = np.concatenate([np.arange(0, SSD_HEADS, 2), np.arange(1, SSD_HEADS, 2)])


def _expand_matrix():
    e = np.zeros((2 * LANES, SSD_D_INNER), np.float32)
    for k, h in enumerate(HEAD_PERM):
        e[k, h * SSD_HEAD_DIM:(h + 1) * SSD_HEAD_DIM] = 1.0
        e[LANES + k, h * SSD_HEAD_DIM:(h + 1) * SSD_HEAD_DIM] = 1.0
    return e


def _layer_norm(x, g, b):
    mu = jnp.mean(x, axis=-1, keepdims=True)
    xc = x - mu
    var = jnp.mean(xc * xc, axis=-1, keepdims=True)
    return xc * lax.rsqrt(var + LN_EPS) * g + b


def _silu(x):
    half = 0.5 * x
    return half + half * jnp.tanh(half)


def _split2(v):
    hi = v.astype(BF16)
    lo = (v - hi.astype(F32)).astype(BF16)
    return jnp.concatenate([hi, lo], axis=1)


def _const_spec(shape):
    nd = len(shape)
    return pl.BlockSpec(shape, lambda *_: (0,) * nd, pipeline_mode=pl.Buffered(1))


def _ssd_stage_a(x_ref, wz_ref, wdt_ref, cw_ref, cb_ref, ext_s, handoff, *, sb_n, tt):
    assert SSD_CONV == 4
    z_w, xs_w, bc_w, dtr_w = handoff
    rows = sb_n * tt
    hist = CONV_PAD - (SSD_CONV - 1)
    live = {}

    def first():
        live["xb"] = x_ref[...].reshape(rows, D_MODEL).astype(BF16)
        dtr_w[...] = jnp.dot(live["xb"], wdt_ref[...], preferred_element_type=F32)

    def dot_unit(c0):
        cs = slice(c0, c0 + COL_BLK)
        raw = jnp.dot(live["xb"], wz_ref[:, SSD_D_INNER + c0:SSD_D_INNER + c0 + COL_BLK],
                      preferred_element_type=F32)
        for sb in range(sb_n):
            ext_s[sb, CONV_PAD:CONV_PAD + tt, cs] = raw[sb * tt:(sb + 1) * tt]

    def conv_unit(c0):
        cs = slice(c0, c0 + COL_BLK)
        for sb in range(sb_n):
            ext = ext_s[sb, :, cs]
            back1 = pltpu.roll(ext, 1, axis=0)
            older = ext * cw_ref[1:2, cs] + back1 * cw_ref[0:1, cs]
            acc = (cb_ref[:, cs] + ext * cw_ref[3:4, cs] + back1 * cw_ref[2:3, cs]
                   + pltpu.roll(older, 2, axis=0))[CONV_PAD:]
            act = _silu(acc)
            rs = slice(sb * tt, (sb + 1) * tt)
            if c0 < SSD_D_INNER:
                xs_w[rs, cs] = act
            else:
                bc_w[rs, c0 - SSD_D_INNER:c0 - SSD_D_INNER + COL_BLK] = act.astype(BF16)
            ext_s[sb, hist:CONV_PAD, cs] = ext[tt + hist:tt + CONV_PAD]

    def z_unit(c0):
        z_w[:, c0:c0 + COL_BLK] = jnp.dot(live["xb"], wz_ref[:, c0:c0 + COL_BLK], preferred_element_type=F32)

    blocks = range(0, SSD_CONV_DIM, COL_BLK)
    return (first,
            [functools.partial(dot_unit, c0) for c0 in blocks],
            [functools.partial(conv_unit, c0) for c0 in blocks],
            [functools.partial(z_unit, c0) for c0 in range(0, SSD_D_INNER, COL_BLK)])


def _ssd_stage_b(handoff, dtb_ref, alog_ref, dexp_ref, nw_ref, e2_ref, yb_w,
                 dtx_s, acx_s, y_s, ac_s, h_s, *, sb_n, tt):
    z_r, xs_r, bc_r, dtr_r = handoff
    rows = sb_n * tt
    nch = tt // CHUNK
    n_pairs = SSD_HEADS // 2
    live = {}

    def prologue():
        v = dtr_r[...] + dtb_ref[...]
        dt = jnp.maximum(v, 0.0) + jnp.log1p(jnp.exp(-jnp.abs(v)))
        a_neg = -jnp.exp(alog_ref[...])
        d_a = dt * a_neg
        ri = lax.broadcasted_iota(jnp.int32, (rows, rows), 0)
        ci = lax.broadcasted_iota(jnp.int32, (rows, rows), 1)
        same_chunk = (ri & -CHUNK) == (ci & -CHUNK)
        tril = jnp.where((ci <= ri) & same_chunk, 1.0, 0.0).astype(BF16)
        cs = jnp.dot(tril, _split2(d_a), preferred_element_type=F32)
        acum = cs[:, 0:LANES] + cs[:, LANES:]
        ac_s[...] = acum
        acx_s[...] = jnp.dot(_split2(acum), e2_ref[...], preferred_element_type=F32)
        dtx_s[...] = xs_r[...] * jnp.dot(_split2(dt), e2_ref[...], preferred_element_type=F32)
        row_i = lax.broadcasted_iota(jnp.int32, (CHUNK, LANES), 0)
        lane_i = lax.broadcasted_iota(jnp.int32, (CHUNK, LANES), 1)
        live["causal2"] = (lane_i & (CHUNK - 1)) <= row_i
        live["first_half"] = lane_i < CHUNK

    def scan_unit(chunk, g0):
        sb = chunk // nch
        r0 = chunk * CHUNK
        rs = slice(r0, r0 + CHUNK)
        if g0 == 0:
            ac_t = ac_s[rs, :].T
            live["pair_rows"] = jnp.concatenate([ac_t[0:n_pairs], ac_t[n_pairs:2 * n_pairs]], axis=1)
        pair_rows, causal2, first_half = live["pair_rows"], live["causal2"], live["first_half"]
        groups = range(g0, g0 + SCAN_UNIT_GROUPS)
        b, c, h, acx, cb2, y_off = {}, {}, {}, {}, {}, {}
        for g in groups:
            gs = slice(g * GROUP_W, (g + 1) * GROUP_W)
            b[g] = bc_r[rs, g * SSD_STATE:(g + 1) * SSD_STATE]
            c[g] = bc_r[rs, SSD_BC + g * SSD_STATE:SSD_BC + (g + 1) * SSD_STATE]
            bb = jnp.concatenate([b[g], b[g]], axis=0)
            cb2[g] = lax.dot_general(c[g], bb, (((1,), (1,)), ((), ())), preferred_element_type=F32)
            h[g] = h_s[sb, g]
            acx[g] = acx_s[rs, gs]
            y_off[g] = jnp.dot(c[g], h[g].astype(BF16), preferred_element_type=F32)
        y_diag = {}
        for g in groups:
            for jj in range(2):
                j = 2 * g + jj
                ls = slice(j * LANES, (j + 1) * LANES)
                seg = acx_s[rs, ls] - pair_rows[j:j + 1, :]
                decay = jnp.where(causal2, jnp.exp(seg), 0.0)
                m2 = (cb2[g] * decay).astype(BF16)
                xp = dtx_s[rs, ls]
                xbd = jnp.concatenate([jnp.where(first_half, xp, 0.0), jnp.where(first_half, 0.0, xp)], axis=0)
                y_diag[j] = jnp.dot(m2, xbd.astype(BF16), preferred_element_type=F32)
        upd, a_last = {}, {}
        for g in groups:
            gs = slice(g * GROUP_W, (g + 1) * GROUP_W)
            a_last[g] = acx_s[r0 + CHUNK - 1:r0 + CHUNK, gs]
            xw = (dtx_s[rs, gs] * jnp.exp(a_last[g] - acx[g])).astype(BF16)
            upd[g] = lax.dot_general(b[g], xw, (((0,), (0,)), ((), ())), preferred_element_type=F32)
        for g in groups:
            y_g = jnp.concatenate([y_diag[2 * g], y_diag[2 * g + 1]], axis=1) + y_off[g] * jnp.exp(acx[g])
            y_s[rs, g * GROUP_W:(g + 1) * GROUP_W] = y_g
            h_s[sb, g] = h[g] * jnp.exp(a_last[g]) + upd[g]

    def gate_unit(g):
        gs = slice(g * NORM_GROUP_W, (g + 1) * NORM_GROUP_W)
        y = y_s[:, gs] + dexp_ref[:, gs] * xs_r[:, gs]
        zg = z_r[:, gs]
        gg = y * _silu(zg)
        ms = jnp.mean(gg * gg, axis=-1, keepdims=True)
        yb_w[:, gs] = (gg * lax.rsqrt(ms + LN_EPS) * nw_ref[:, gs]).astype(BF16)

    return (prologue,
            [functools.partial(scan_unit, c, g) for c in range(sb_n * nch)
             for g in range(0, SSD_GROUPS, SCAN_UNIT_GROUPS)],
            [functools.partial(gate_unit, g) for g in range(SSD_GROUPS)])


def _ssd_stage_c(yb_r, x_ref, wout_ref, g_ref, b_ref, o_ref, *, sb_n, tt):
    rows = sb_n * tt

    def out_unit(c0):
        mix = jnp.dot(yb_r[...], wout_ref[:, c0:c0 + OUT_BLK], preferred_element_type=F32)
        o_ref[:, :, c0:c0 + OUT_BLK] = mix.reshape(sb_n, tt, OUT_BLK)

    def last():
        mix = o_ref[...].reshape(rows, D_MODEL)
        x = x_ref[...].reshape(rows, D_MODEL)
        o_ref[...] = _layer_norm(DN_ALPHA * x + mix, g_ref[...], b_ref[...]).reshape(sb_n, tt, D_MODEL)

    return [functools.partial(out_unit, c0) for c0 in range(0, D_MODEL, OUT_BLK)], last


def _ssd_trace_order(stage_a, stage_b, stage_c):
    a_first, a_dots, a_convs, a_z = stage_a
    b_pro, b_scans, b_gates = stage_b
    c_units = list(stage_c[0]) + [stage_c[1]]
    lead = 4
    n_blk = len(a_dots)
    order = [a_first, a_dots[0], a_dots[1], b_pro, a_dots[2], a_dots[3]]
    nxt_conv, nxt_dot = 0, lead

    def conv_and_dot():
        nonlocal nxt_conv, nxt_dot
        order.append(a_convs[nxt_conv])
        nxt_conv += 1
        if nxt_dot < n_blk:
            order.append(a_dots[nxt_dot])
            nxt_dot += 1

    n_scan = len(b_scans)
    n_beside_scans = 10 * n_scan * SCAN_UNIT_GROUPS // 32
    n_c = len(c_units)
    for u, scan in enumerate(b_scans):
        order.append(scan)
        for _ in range((u + 1) * n_beside_scans // n_scan - u * n_beside_scans // n_scan):
            conv_and_dot()
        for _ in range((u + 1) * n_c // n_scan - u * n_c // n_scan):
            order.append(c_units.pop(0))
    for g, gate in enumerate(b_gates):
        order.append(gate)
        order.extend(a_z[g:g + 1])
        if nxt_conv < n_blk:
            conv_and_dot()
    order.extend(a_z[len(b_gates):])
    while nxt_conv < n_blk:
        conv_and_dot()
    order.extend(c_units)
    n_units = 1 + 2 * n_blk + len(a_z) + 1 + len(b_scans) + len(b_gates) + len(stage_c[0]) + 1
    assert len(order) == n_units and len(set(map(id, order))) == n_units
    return order


def _ssd_kernel(*refs, sb_n, tt, nt, n_tiles, has_state):
    if has_state:
        (x_ref, xres_ref, cprev_ref, h0_ref, *refs) = refs
    else:
        (x_ref, xres_ref, *refs) = refs
    (wz_ref, wdt_ref, cw_ref, cb_ref, dtb_ref, alog_ref, dexp_ref, nw_ref, wout_ref, e2_ref,
     g_ref, b_ref, o_ref, cnew_ref, hout_ref,
     ext_s, dtx_s, acx_s, y_s, ac_s, h_s, yb0_s, yb1_s, *handoffs) = refs
    n_hand = len(handoffs) // 2
    hand = (tuple(handoffs[:n_hand]), tuple(handoffs[n_hand:]))
    yb = (yb0_s, yb1_s)
    i = pl.program_id(0)
    t_a = jnp.minimum(i, n_tiles - 1) % nt
    t_b = jnp.clip(i - 1, 0, n_tiles - 1) % nt
    hist = CONV_PAD - (SSD_CONV - 1)

    @pl.when(i == 0)
    def _zero_first_handoff():
        for ref in hand[1] + (yb[0],):
            ref[...] = jnp.zeros(ref.shape, ref.dtype)

    @pl.when(t_a == 0)
    def _conv_history():
        ext_s[:, 0:CONV_PAD, :] = jnp.zeros((sb_n, CONV_PAD, SSD_CONV_DIM), F32)
        if has_state:
            for sb in range(sb_n):
                ext_s[sb, hist:CONV_PAD, :] = cprev_ref[sb]

    @pl.when(t_b == 0)
    def _state_in():
        if has_state:
            for sb in range(sb_n):
                for g in range(SSD_GROUPS):
                    hg = h0_ref[sb, g * HEADS_PER_GROUP:(g + 1) * HEADS_PER_GROUP]
                    h_s[sb, g] = hg.reshape(GROUP_W, SSD_STATE).T
        else:
            h_s[...] = jnp.zeros(h_s.shape, F32)

    def all_stages(parity):
        stage_a = _ssd_stage_a(x_ref, wz_ref, wdt_ref, cw_ref, cb_ref, ext_s, hand[parity], sb_n=sb_n, tt=tt)
        stage_b = _ssd_stage_b(hand[1 - parity], dtb_ref, alog_ref, dexp_ref, nw_ref, e2_ref, yb[1 - parity],
                               dtx_s, acx_s, y_s, ac_s, h_s, sb_n=sb_n, tt=tt)
        stage_c = _ssd_stage_c(yb[parity], xres_ref, wout_ref, g_ref, b_ref, o_ref, sb_n=sb_n, tt=tt)
        for unit in _ssd_trace_order(stage_a, stage_b, stage_c):
            unit()

    @pl.when(i % 2 == 0)
    def _even():
        all_stages(0)

    @pl.when(i % 2 == 1)
    def _odd():
        all_stages(1)

    @pl.when((t_a == nt - 1) & (i < n_tiles))
    def _conv_out():
        cnew_ref[...] = ext_s[:, hist:CONV_PAD, :]

    @pl.when((t_b == nt - 1) & (i >= 1) & (i <= n_tiles))
    def _state_out():
        for sb in range(sb_n):
            for g in range(SSD_GROUPS):
                hout_ref[sb, g * HEADS_PER_GROUP:(g + 1) * HEADS_PER_GROUP] = (
                    h_s[sb, g].T.reshape(HEADS_PER_GROUP, SSD_HEAD_DIM, SSD_STATE))


def _ssd_layer(x, conv_prev, h0, w, *, sb_n, tt):
    nseq, seq_len, _ = x.shape
    has_state = h0 is not None
    rows = sb_n * tt
    nt = seq_len // tt
    n_tiles = (nseq // sb_n) * nt

    def tile_a(i):
        return jnp.minimum(i, n_tiles - 1)

    def tile_b(i):
        return jnp.clip(i - 1, 0, n_tiles - 1)

    def tile_c(i):
        return jnp.maximum(i - 2, 0)

    x_spec = pl.BlockSpec((sb_n, tt, D_MODEL), lambda i: (tile_a(i) // nt, tile_a(i) % nt, 0))
    o_spec = pl.BlockSpec((sb_n, tt, D_MODEL), lambda i: (tile_c(i) // nt, tile_c(i) % nt, 0))
    conv_spec = pl.BlockSpec((sb_n, SSD_CONV - 1, SSD_CONV_DIM), lambda i: (tile_a(i) // nt, 0, 0))
    state_spec = pl.BlockSpec((sb_n, SSD_HEADS, SSD_HEAD_DIM, SSD_STATE), lambda i: (tile_b(i) // nt, 0, 0, 0))
    consts = (w["w_zx"], w["w_dt"], w["conv_w"], w["conv_b"], w["dt_bias"], w["a_log"], w["d_exp"],
              w["norm_w"], w["w_out"], w["e2"], w["ln_g"], w["ln_b"])
    in_specs = ([x_spec, o_spec] + ([conv_spec, state_spec] if has_state else [])
                + [_const_spec(c.shape) for c in consts])
    args = (x, x) + ((conv_prev, h0) if has_state else ()) + consts
    handoff = [
        pltpu.VMEM((rows, SSD_D_INNER), F32),
        pltpu.VMEM((rows, SSD_D_INNER), F32),
        pltpu.VMEM((rows, 2 * SSD_BC), BF16),
        pltpu.VMEM((rows, LANES), F32),
    ]
    scratch = [
        pltpu.VMEM((sb_n, tt + CONV_PAD, SSD_CONV_DIM), F32),
        pltpu.VMEM((rows, SSD_D_INNER), F32),
        pltpu.VMEM((rows, SSD_D_INNER), F32),
        pltpu.VMEM((rows, SSD_D_INNER), F32),
        pltpu.VMEM((rows, LANES), F32),
        pltpu.VMEM((sb_n, SSD_GROUPS, SSD_STATE, GROUP_W), F32),
        pltpu.VMEM((rows, SSD_D_INNER), BF16),
        pltpu.VMEM((rows, SSD_D_INNER), BF16),
    ] + handoff + handoff
    out_shape = (
        jax.ShapeDtypeStruct(x.shape, F32),
        jax.ShapeDtypeStruct((nseq, SSD_CONV - 1, SSD_CONV_DIM), F32),
        jax.ShapeDtypeStruct((nseq, SSD_HEADS, SSD_HEAD_DIM, SSD_STATE), F32),
    )
    return pl.pallas_call(
        functools.partial(_ssd_kernel, sb_n=sb_n, tt=tt, nt=nt, n_tiles=n_tiles, has_state=has_state),
        grid=(n_tiles + 2,),
        in_specs=in_specs,
        out_specs=(o_spec, conv_spec, state_spec),
        out_shape=out_shape,
        scratch_shapes=scratch,
        compiler_params=pltpu.CompilerParams(dimension_semantics=("arbitrary",), vmem_limit_bytes=VMEM_LIMIT),
        name="ssd_state" if has_state else "ssd_fresh",
    )(*args)


def _ffn_kernel(x_ref, w1_ref, w2_ref, g_ref, b_ref, o_ref):
    x = x_ref[...]
    h = jnp.dot(x.astype(BF16), w1_ref[...], preferred_element_type=F32)
    h = jnp.maximum(h, 0.0)
    y = jnp.dot((h * h).astype(BF16), w2_ref[...], preferred_element_type=F32)
    o_ref[...] = _layer_norm(DN_ALPHA * x + y, g_ref[...], b_ref[...])


def _ffn_layer(x2d, w1, w2, g, b, *, tm):
    n = x2d.shape[0]
    row_spec = pl.BlockSpec((tm, D_MODEL), lambda i: (i, 0))
    return pl.pallas_call(
        _ffn_kernel,
        grid=(n // tm,),
        in_specs=[row_spec, _const_spec(w1.shape), _const_spec(w2.shape), _const_spec(g.shape), _const_spec(b.shape)],
        out_specs=row_spec,
        out_shape=jax.ShapeDtypeStruct(x2d.shape, F32),
        compiler_params=pltpu.CompilerParams(dimension_semantics=("arbitrary",), vmem_limit_bytes=VMEM_LIMIT),
        name="ffn",
    )(x2d, w1, w2, g, b)


def _cmlp_kernel(x_ref, win_ref, bin_ref, lng_ref, lnb_ref, ws_ref, bs_ref, wout_ref, g_ref, b_ref,
                 o_ref, *rest, blk, emit_v):
    if emit_v:
        v_ref, u_s, vpre_s, v_s = rest
    else:
        u_s, vpre_s, v_s = rest
    x = x_ref[...]
    tm = x.shape[0]
    xb = x.astype(BF16)

    def gelu_block(c0):
        cs = slice(c0, c0 + COL_BLK_CM)
        h = jnp.dot(xb, win_ref[:, cs], preferred_element_type=F32) + bin_ref[:, cs]
        return 0.5 * h * (1.0 + jnp.tanh(0.7978845608028654 * (h + 0.044715 * (h * h * h))))

    for c0 in range(0, CM_WIDTH, COL_BLK_CM):
        vpre_s[:, c0:c0 + COL_BLK_CM] = gelu_block(CM_WIDTH + c0)
    v = _layer_norm(vpre_s[...], lng_ref[...], lnb_ref[...])
    if emit_v:
        v_ref[...] = v
    v_s[...] = v.astype(BF16)
    for c0 in range(0, CM_WIDTH, COL_BLK_CM):
        u_s[:, c0:c0 + COL_BLK_CM] = gelu_block(c0)
    ri = lax.broadcasted_iota(jnp.int32, (blk, blk), 0)
    ci = lax.broadcasted_iota(jnp.int32, (blk, blk), 1)
    for g in range(CM_GROUPS):
        gs = slice(g * CM_GROUP_W, (g + 1) * CM_GROUP_W)
        ws_g = jnp.where(ci <= ri, ws_ref[g, 0:blk, 0:blk], 0.0).astype(BF16)
        for i in range(tm // blk):
            rs = slice(i * blk, (i + 1) * blk)
            s = jnp.dot(ws_g, v_s[rs, gs], preferred_element_type=F32) + bs_ref[0:blk, gs]
            u_s[rs, gs] = u_s[rs, gs] * s
    mix = jnp.dot(u_s[...].astype(BF16), wout_ref[...], preferred_element_type=F32)
    o_ref[...] = _layer_norm(DN_ALPHA * x + mix, g_ref[...], b_ref[...])


def _cmlp_layer(x2d, w, *, tm, blk, emit_v):
    n = x2d.shape[0]
    row_spec = pl.BlockSpec((tm, D_MODEL), lambda i: (i, 0))
    consts = (w["w_in"], w["b_in"], w["ln_g"], w["ln_b"], w["w_s"], w["b_s"], w["w_out"], w["ln1_g"], w["ln1_b"])
    out_shape = [jax.ShapeDtypeStruct(x2d.shape, F32)]
    out_specs = [row_spec]
    if emit_v:
        out_shape.append(jax.ShapeDtypeStruct((n, CM_WIDTH), F32))
        out_specs.append(pl.BlockSpec((tm, CM_WIDTH), lambda i: (i, 0)))
    return pl.pallas_call(
        functools.partial(_cmlp_kernel, blk=blk, emit_v=emit_v),
        grid=(n // tm,),
        in_specs=[row_spec] + [_const_spec(c.shape) for c in consts],
        out_specs=out_specs,
        out_shape=out_shape,
        scratch_shapes=[pltpu.VMEM((tm, CM_WIDTH), F32), pltpu.VMEM((tm, CM_WIDTH), F32),
                        pltpu.VMEM((tm, CM_WIDTH), BF16)],
        compiler_params=pltpu.CompilerParams(dimension_semantics=("arbitrary",), vmem_limit_bytes=VMEM_LIMIT),
        name="cmlp_v" if emit_v else "cmlp",
    )(x2d, *consts)


def kernel(x_prompt, x_sample, state_ssm, state_conv, ssd_w_in, ssd_conv_w, ssd_conv_b, ssd_dt_bias, ssd_a_log, ssd_d, ssd_norm_w, ssd_w_out, cm_w_in, cm_b_in, cm_ln_g, cm_ln_b, cm_w_s, cm_b_s, cm_w_out, ffn_w1, ffn_w2, ln1_g, ln1_b, ln2_g, ln2_b):
    row = lambda a: a.reshape(1, -1)
    perm = jnp.asarray(HEAD_PERM)

    def head_lanes(a):
        return jnp.pad(a[perm], (0, LANES - SSD_HEADS)).reshape(1, LANES)

    w_in = ssd_w_in[0]
    zx_end = SSD_D_INNER + SSD_CONV_DIM
    ssd_w = {
        "w_zx": w_in[:, 0:zx_end].astype(BF16),
        "w_dt": jnp.pad(w_in[:, zx_end:][:, perm], ((0, 0), (0, LANES - SSD_HEADS))).astype(BF16),
        "conv_w": ssd_conv_w[0],
        "conv_b": row(ssd_conv_b[0]),
        "dt_bias": head_lanes(ssd_dt_bias[0]),
        "a_log": head_lanes(ssd_a_log[0]),
        "d_exp": row(jnp.repeat(ssd_d[0], SSD_HEAD_DIM)),
        "norm_w": row(ssd_norm_w[0]),
        "w_out": ssd_w_out[0].astype(BF16),
        "e2": jnp.asarray(_expand_matrix(), BF16),
        "ln_g": row(ln1_g[0]),
        "ln_b": row(ln1_b[0]),
    }
    cm_w = {
        "w_in": cm_w_in[0].astype(BF16),
        "b_in": row(cm_b_in[0]),
        "ln_g": row(cm_ln_g[0]),
        "ln_b": row(cm_ln_b[0]),
        "w_s": cm_w_s[0],
        "b_s": jnp.repeat(cm_b_s[0].T, CM_GROUP_W, axis=1),
        "w_out": cm_w_out[0].astype(BF16),
        "ln1_g": row(ln1_g[1]),
        "ln1_b": row(ln1_b[1]),
    }
    ffn = [(ffn_w1[i].astype(BF16), ffn_w2[i].astype(BF16), row(ln2_g[i]), row(ln2_b[i])) for i in range(DEPTH)]

    def trunk(x, conv_prev, h0, *, sb_n, tt, blk, emit_v):
        shape = x.shape
        x1, conv_new, h_new = _ssd_layer(x, conv_prev, h0, ssd_w, sb_n=sb_n, tt=tt)
        x2 = _ffn_layer(x1.reshape(-1, D_MODEL), *ffn[0], tm=512)
        res = _cmlp_layer(x2, cm_w, tm=512, blk=blk, emit_v=emit_v)
        x4 = _ffn_layer(res[0], *ffn[1], tm=512)
        v_rows = res[1].reshape(shape[0], shape[1], CM_WIDTH) if emit_v else None
        return x4.reshape(shape), conv_new, h_new, v_rows

    y_p, conv_p, h_p, _ = trunk(x_prompt, None, None, sb_n=1, tt=256, blk=CM_BLOCK, emit_v=False)
    y_s, conv_s, h_s, v_s = trunk(x_sample, state_conv[0], state_ssm[0], sb_n=2, tt=CHUNK,
                                  blk=x_sample.shape[1], emit_v=True)
    return (y_p, y_s, h_p[None], conv_p[None], h_s[None], conv_s[None], v_s[None])
```

```python
import functools

import numpy as np
import jax
import jax.numpy as jnp
from jax import lax
from jax.experimental import pallas as pl
from jax.experimental.pallas import tpu as pltpu

F32 = jnp.float32
BF16 = jnp.bfloat16

D_MODEL = 1024
DEPTH = 2
CHUNK = 64
SSD_D_INNER = 2 * D_MODEL
SSD_HEAD_DIM = 64
SSD_HEADS = SSD_D_INNER // SSD_HEAD_DIM
SSD_GROUPS = 8
SSD_STATE = 128
SSD_CONV = 4
SSD_BC = SSD_GROUPS * SSD_STATE
SSD_CONV_DIM = SSD_D_INNER + 2 * SSD_BC
HEADS_PER_GROUP = SSD_HEADS // SSD_GROUPS
GROUP_W = HEADS_PER_GROUP * SSD_HEAD_DIM
NORM_GROUP_W = SSD_D_INNER // SSD_GROUPS
CM_BLOCK = 128
CM_WIDTH = 2 * D_MODEL
CM_GROUPS = 8
CM_GROUP_W = CM_WIDTH // CM_GROUPS
FFN_HIDDEN = 4 * D_MODEL
DN_ALPHA = (2 * DEPTH) ** 0.25
LN_EPS = 1e-5
LOG2_E = 1.4426950408889634

LANES = 128
CONV_PAD = 8
COL_BLK = 256
OUT_BLK = 256
COL_BLK_CM = 512
SCAN_UNIT_GROUPS = 4
VMEM_LIMIT = 56 * 1024 * 1024

HEAD_PERM = np.concatenate([np.arange(0, SSD_HEADS, 2), np.arange(1, SSD_HEADS, 2)])


def _expand_matrix():
    e = np.zeros((2 * LANES, SSD_D_INNER), np.float32)
    for k, h in enumerate(HEAD_PERM):
        e[k, h * SSD_HEAD_DIM:(h + 1) * SSD_HEAD_DIM] = 1.0
        e[LANES + k, h * SSD_HEAD_DIM:(h + 1) * SSD_HEAD_DIM] = 1.0
    return e


def _layer_norm(x, g, b):
    mu = jnp.mean(x, axis=-1, keepdims=True)
    xc = x - mu
    var = jnp.mean(xc * xc, axis=-1, keepdims=True)
    return xc * lax.rsqrt(var + LN_EPS) * g + b


def _silu(x):
    half = 0.5 * x
    return half + half * jnp.tanh(half)


def _split2(v):
    hi = v.astype(BF16)
    lo = (v - hi.astype(F32)).astype(BF16)
    return jnp.concatenate([hi, lo], axis=1)


def _const_spec(shape):
    nd = len(shape)
    return pl.BlockSpec(shape, lambda *_: (0,) * nd, pipeline_mode=pl.Buffered(1))


def _ssd_stage_a(x_ref, wz_ref, wdt_ref, cw_ref, cb_ref, ext_s, handoff, *, sb_n, tt):
    assert SSD_CONV == 4
    z_w, xs_w, bc_w, dtr_w = handoff
    rows = sb_n * tt
    hist = CONV_PAD - (SSD_CONV - 1)
    live = {}

    def first():
        live["xb"] = x_ref[...].reshape(rows, D_MODEL).astype(BF16)
        dtr_w[...] = jnp.dot(live["xb"], wdt_ref[...], preferred_element_type=F32)

    def dot_unit(c0):
        cs = slice(c0, c0 + COL_BLK)
        raw = jnp.dot(live["xb"], wz_ref[:, SSD_D_INNER + c0:SSD_D_INNER + c0 + COL_BLK],
                      preferred_element_type=F32)
        for sb in range(sb_n):
            ext_s[sb, CONV_PAD:CONV_PAD + tt, cs] = raw[sb * tt:(sb + 1) * tt]

    def conv_unit(c0):
        cs = slice(c0, c0 + COL_BLK)
        for sb in range(sb_n):
            ext = ext_s[sb, :, cs]
            back1 = pltpu.roll(ext, 1, axis=0)
            older = ext * cw_ref[1:2, cs] + back1 * cw_ref[0:1, cs]
            acc = (cb_ref[:, cs] + ext * cw_ref[3:4, cs] + back1 * cw_ref[2:3, cs]
                   + pltpu.roll(older, 2, axis=0))[CONV_PAD:]
            act = _silu(acc)
            rs = slice(sb * tt, (sb + 1) * tt)
            if c0 < SSD_D_INNER:
                xs_w[rs, cs] = act
            else:
                bc_w[rs, c0 - SSD_D_INNER:c0 - SSD_D_INNER + COL_BLK] = act.astype(BF16)
            ext_s[sb, hist:CONV_PAD, cs] = ext[tt + hist:tt + CONV_PAD]

    def z_unit(c0):
        z_w[:, c0:c0 + COL_BLK] = jnp.dot(live["xb"], wz_ref[:, c0:c0 + COL_BLK], preferred_element_type=F32)

    blocks = range(0, SSD_CONV_DIM, COL_BLK)
    return (first,
            [functools.partial(dot_unit, c0) for c0 in blocks],
            [functools.partial(conv_unit, c0) for c0 in blocks],
            [functools.partial(z_unit, c0) for c0 in range(0, SSD_D_INNER, COL_BLK)])


def _ssd_stage_b(handoff, dtb_ref, alog_ref, dexp_ref, nw_ref, e2_ref, yb_w,
                 dtx_s, acx_s, y_s, ac_s, h_s, *, sb_n, tt):
    z_r, xs_r, bc_r, dtr_r = handoff
    rows = sb_n * tt
    nch = tt // CHUNK
    n_pairs = SSD_HEADS // 2
    live = {}

    def prologue():
        v = dtr_r[...] + dtb_ref[...]
        dt = jnp.maximum(v, 0.0) + jnp.log1p(jnp.exp(-jnp.abs(v)))
        a_neg = -jnp.exp(alog_ref[...])
        d_a = dt * a_neg
        ri = lax.broadcasted_iota(jnp.int32, (rows, rows), 0)
        ci = lax.broadcasted_iota(jnp.int32, (rows, rows), 1)
        same_chunk = (ri & -CHUNK) == (ci & -CHUNK)
        tril = jnp.where((ci <= ri) & same_chunk, 1.0, 0.0).astype(BF16)
        cs = jnp.dot(tril, _split2(d_a), preferred_element_type=F32)
        acum = (cs[:, 0:LANES] + cs[:, LANES:]) * LOG2_E
        ac_s[...] = acum
        acx_s[...] = jnp.dot(_split2(acum), e2_ref[...], preferred_element_type=F32)
        dtx_s[...] = xs_r[...] * jnp.dot(_split2(dt), e2_ref[...], preferred_element_type=F32)
        row_i = lax.broadcasted_iota(jnp.int32, (CHUNK, LANES), 0)
        lane_i = lax.broadcasted_iota(jnp.int32, (CHUNK, LANES), 1)
        live["causal2"] = (lane_i & (CHUNK - 1)) <= row_i
        live["first_half"] = lane_i < CHUNK

    def scan_unit(chunk, g0):
        sb = chunk // nch
        r0 = chunk * CHUNK
        rs = slice(r0, r0 + CHUNK)
        if g0 == 0:
            ac_t = ac_s[rs, :].T
            live["pair_rows"] = jnp.concatenate([ac_t[0:n_pairs], ac_t[n_pairs:2 * n_pairs]], axis=1)
        pair_rows, causal2, first_half = live["pair_rows"], live["causal2"], live["first_half"]
        groups = range(g0, g0 + SCAN_UNIT_GROUPS)
        b, c, h, acx, cb2, y_off = {}, {}, {}, {}, {}, {}
        for g in groups:
            gs = slice(g * GROUP_W, (g + 1) * GROUP_W)
            b[g] = bc_r[rs, g * SSD_STATE:(g + 1) * SSD_STATE]
            c[g] = bc_r[rs, SSD_BC + g * SSD_STATE:SSD_BC + (g + 1) * SSD_STATE]
            bb = jnp.concatenate([b[g], b[g]], axis=0)
            cb2[g] = lax.dot_general(c[g], bb, (((1,), (1,)), ((), ())), preferred_element_type=F32)
            h[g] = h_s[sb, g]
            acx[g] = acx_s[rs, gs]
            y_off[g] = jnp.dot(c[g], h[g].astype(BF16), preferred_element_type=F32)
        y_diag = {}
        for g in groups:
            for jj in range(2):
                j = 2 * g + jj
                ls = slice(j * LANES, (j + 1) * LANES)
                seg = acx_s[rs, ls] - pair_rows[j:j + 1, :]
                decay = jnp.where(causal2, jnp.exp2(seg), 0.0)
                m2 = (cb2[g] * decay).astype(BF16)
                xp = dtx_s[rs, ls]
                xbd = jnp.concatenate([jnp.where(first_half, xp, 0.0), jnp.where(first_half, 0.0, xp)], axis=0)
                y_diag[j] = jnp.dot(m2, xbd.astype(BF16), preferred_element_type=F32)
        upd, a_last = {}, {}
        for g in groups:
            gs = slice(g * GROUP_W, (g + 1) * GROUP_W)
            a_last[g] = acx_s[r0 + CHUNK - 1:r0 + CHUNK, gs]
            xw = (dtx_s[rs, gs] * jnp.exp2(a_last[g] - acx[g])).astype(BF16)
            upd[g] = lax.dot_general(b[g], xw, (((0,), (0,)), ((), ())), preferred_element_type=F32)
        for g in groups:
            y_g = jnp.concatenate([y_diag[2 * g], y_diag[2 * g + 1]], axis=1) + y_off[g] * jnp.exp2(acx[g])
            y_s[rs, g * GROUP_W:(g + 1) * GROUP_W] = y_g
            h_s[sb, g] = h[g] * jnp.exp2(a_last[g]) + upd[g]

    def gate_unit(g):
        gs = slice(g * NORM_GROUP_W, (g + 1) * NORM_GROUP_W)
        y = y_s[:, gs] + dexp_ref[:, gs] * xs_r[:, gs]
        zg = z_r[:, gs]
        gg = y * _silu(zg)
        ms = jnp.mean(gg * gg, axis=-1, keepdims=True)
        yb_w[:, gs] = (gg * lax.rsqrt(ms + LN_EPS) * nw_ref[:, gs]).astype(BF16)

    return (prologue,
            [functools.partial(scan_unit, c, g) for c in range(sb_n * nch)
             for g in range(0, SSD_GROUPS, SCAN_UNIT_GROUPS)],
            [functools.partial(gate_unit, g) for g in range(SSD_GROUPS)])


def _ssd_stage_c(yb_r, x_ref, wout_ref, g_ref, b_ref, o_ref, *, sb_n, tt):
    rows = sb_n * tt

    def out_unit(c0):
        mix = jnp.dot(yb_r[...], wout_ref[:, c0:c0 + OUT_BLK], preferred_element_type=F32)
        o_ref[:, :, c0:c0 + OUT_BLK] = mix.reshape(sb_n, tt, OUT_BLK)

    def last():
        mix = o_ref[...].reshape(rows, D_MODEL)
        x = x_ref[...].reshape(rows, D_MODEL)
        o_ref[...] = _layer_norm(DN_ALPHA * x + mix, g_ref[...], b_ref[...]).reshape(sb_n, tt, D_MODEL)

    return [functools.partial(out_unit, c0) for c0 in range(0, D_MODEL, OUT_BLK)], last


def _ssd_trace_order(stage_a, stage_b, stage_c):
    a_first, a_dots, a_convs, a_z = stage_a
    b_pro, b_scans, b_gates = stage_b
    c_units = list(stage_c[0]) + [stage_c[1]]
    lead = 4
    n_blk = len(a_dots)
    order = [a_first, a_dots[0], a_dots[1], b_pro, a_dots[2], a_dots[3]]
    nxt_conv, nxt_dot = 0, lead

    def conv_and_dot():
        nonlocal nxt_conv, nxt_dot
        order.append(a_convs[nxt_conv])
        nxt_conv += 1
        if nxt_dot < n_blk:
            order.append(a_dots[nxt_dot])
            nxt_dot += 1

    n_scan = len(b_scans)
    n_beside_scans = 10 * n_scan * SCAN_UNIT_GROUPS // 32
    n_c = len(c_units)
    for u, scan in enumerate(b_scans):
        order.append(scan)
        for _ in range((u + 1) * n_beside_scans // n_scan - u * n_beside_scans // n_scan):
            conv_and_dot()
        for _ in range((u + 1) * n_c // n_scan - u * n_c // n_scan):
            order.append(c_units.pop(0))
    for g, gate in enumerate(b_gates):
        order.append(gate)
        order.extend(a_z[g:g + 1])
        if nxt_conv < n_blk:
            conv_and_dot()
    order.extend(a_z[len(b_gates):])
    while nxt_conv < n_blk:
        conv_and_dot()
    order.extend(c_units)
    n_units = 1 + 2 * n_blk + len(a_z) + 1 + len(b_scans) + len(b_gates) + len(stage_c[0]) + 1
    assert len(order) == n_units and len(set(map(id, order))) == n_units
    return order


def _ssd_kernel(*refs, sb_n, tt, nt, n_tiles, has_state):
    if has_state:
        (x_ref, xres_ref, cprev_ref, h0_ref, *refs) = refs
    else:
        (x_ref, xres_ref, *refs) = refs
    (wz_ref, wdt_ref, cw_ref, cb_ref, dtb_ref, alog_ref, dexp_ref, nw_ref, wout_ref, e2_ref,
     g_ref, b_ref, o_ref, cnew_ref, hout_ref,
     ext_s, dtx_s, acx_s, y_s, ac_s, h_s, yb0_s, yb1_s, *handoffs) = refs
    n_hand = len(handoffs) // 2
    hand = (tuple(handoffs[:n_hand]), tuple(handoffs[n_hand:]))
    yb = (yb0_s, yb1_s)
    i = pl.program_id(0)
    t_a = jnp.minimum(i, n_tiles - 1) % nt
    t_b = jnp.clip(i - 1, 0, n_tiles - 1) % nt
    hist = CONV_PAD - (SSD_CONV - 1)

    @pl.when(i == 0)
    def _zero_first_handoff():
        for ref in hand[1] + (yb[0],):
            ref[...] = jnp.zeros(ref.shape, ref.dtype)

    @pl.when(t_a == 0)
    def _conv_history():
        ext_s[:, 0:CONV_PAD, :] = jnp.zeros((sb_n, CONV_PAD, SSD_CONV_DIM), F32)
        if has_state:
            for sb in range(sb_n):
                ext_s[sb, hist:CONV_PAD, :] = cprev_ref[sb]

    @pl.when(t_b == 0)
    def _state_in():
        if has_state:
            for sb in range(sb_n):
                for g in range(SSD_GROUPS):
                    hg = h0_ref[sb, g * HEADS_PER_GROUP:(g + 1) * HEADS_PER_GROUP]
                    h_s[sb, g] = hg.reshape(GROUP_W, SSD_STATE).T
        else:
            h_s[...] = jnp.zeros(h_s.shape, F32)

    def all_stages(parity):
        stage_a = _ssd_stage_a(x_ref, wz_ref, wdt_ref, cw_ref, cb_ref, ext_s, hand[parity], sb_n=sb_n, tt=tt)
        stage_b = _ssd_stage_b(hand[1 - parity], dtb_ref, alog_ref, dexp_ref, nw_ref, e2_ref, yb[1 - parity],
                               dtx_s, acx_s, y_s, ac_s, h_s, sb_n=sb_n, tt=tt)
        stage_c = _ssd_stage_c(yb[parity], xres_ref, wout_ref, g_ref, b_ref, o_ref, sb_n=sb_n, tt=tt)
        for unit in _ssd_trace_order(stage_a, stage_b, stage_c):
            unit()

    @pl.when(i % 2 == 0)
    def _even():
        all_stages(0)

    @pl.when(i % 2 == 1)
    def _odd():
        all_stages(1)

    @pl.when((t_a == nt - 1) & (i < n_tiles))
    def _conv_out():
        cnew_ref[...] = ext_s[:, hist:CONV_PAD, :]

    @pl.when((t_b == nt - 1) & (i >= 1) & (i <= n_tiles))
    def _state_out():
        for sb in range(sb_n):
            for g in range(SSD_GROUPS):
                hout_ref[sb, g * HEADS_PER_GROUP:(g + 1) * HEADS_PER_GROUP] = (
                    h_s[sb, g].T.reshape(HEADS_PER_GROUP, SSD_HEAD_DIM, SSD_STATE))


def _ssd_layer(x, conv_prev, h0, w, *, sb_n, tt):
    nseq, seq_len, _ = x.shape
    has_state = h0 is not None
    rows = sb_n * tt
    nt = seq_len // tt
    n_tiles = (nseq // sb_n) * nt

    def tile_a(i):
        return jnp.minimum(i, n_tiles - 1)

    def tile_b(i):
        return jnp.clip(i - 1, 0, n_tiles - 1)

    def tile_c(i):
        return jnp.maximum(i - 2, 0)

    x_spec = pl.BlockSpec((sb_n, tt, D_MODEL), lambda i: (tile_a(i) // nt, tile_a(i) % nt, 0))
    o_spec = pl.BlockSpec((sb_n, tt, D_MODEL), lambda i: (tile_c(i) // nt, tile_c(i) % nt, 0))
    conv_spec = pl.BlockSpec((sb_n, SSD_CONV - 1, SSD_CONV_DIM), lambda i: (tile_a(i) // nt, 0, 0))
    state_spec = pl.BlockSpec((sb_n, SSD_HEADS, SSD_HEAD_DIM, SSD_STATE), lambda i: (tile_b(i) // nt, 0, 0, 0))
    consts = (w["w_zx"], w["w_dt"], w["conv_w"], w["conv_b"], w["dt_bias"], w["a_log"], w["d_exp"],
              w["norm_w"], w["w_out"], w["e2"], w["ln_g"], w["ln_b"])
    in_specs = ([x_spec, o_spec] + ([conv_spec, state_spec] if has_state else [])
                + [_const_spec(c.shape) for c in consts])
    args = (x, x) + ((conv_prev, h0) if has_state else ()) + consts
    handoff = [
        pltpu.VMEM((rows, SSD_D_INNER), F32),
        pltpu.VMEM((rows, SSD_D_INNER), F32),
        pltpu.VMEM((rows, 2 * SSD_BC), BF16),
        pltpu.VMEM((rows, LANES), F32),
    ]
    scratch = [
        pltpu.VMEM((sb_n, tt + CONV_PAD, SSD_CONV_DIM), F32),
        pltpu.VMEM((rows, SSD_D_INNER), F32),
        pltpu.VMEM((rows, SSD_D_INNER), F32),
        pltpu.VMEM((rows, SSD_D_INNER), F32),
        pltpu.VMEM((rows, LANES), F32),
        pltpu.VMEM((sb_n, SSD_GROUPS, SSD_STATE, GROUP_W), F32),
        pltpu.VMEM((rows, SSD_D_INNER), BF16),
        pltpu.VMEM((rows, SSD_D_INNER), BF16),
    ] + handoff + handoff
    out_shape = (
        jax.ShapeDtypeStruct(x.shape, F32),
        jax.ShapeDtypeStruct((nseq, SSD_CONV - 1, SSD_CONV_DIM), F32),
        jax.ShapeDtypeStruct((nseq, SSD_HEADS, SSD_HEAD_DIM, SSD_STATE), F32),
    )
    return pl.pallas_call(
        functools.partial(_ssd_kernel, sb_n=sb_n, tt=tt, nt=nt, n_tiles=n_tiles, has_state=has_state),
        grid=(n_tiles + 2,),
        in_specs=in_specs,
        out_specs=(o_spec, conv_spec, state_spec),
        out_shape=out_shape,
        scratch_shapes=scratch,
        compiler_params=pltpu.CompilerParams(dimension_semantics=("arbitrary",), vmem_limit_bytes=VMEM_LIMIT),
        name="ssd_state" if has_state else "ssd_fresh",
    )(*args)


def _ffn_kernel(x_ref, w1_ref, w2_ref, g_ref, b_ref, o_ref):
    x = x_ref[...]
    h = jnp.dot(x.astype(BF16), w1_ref[...], preferred_element_type=F32)
    h = jnp.maximum(h, 0.0)
    y = jnp.dot((h * h).astype(BF16), w2_ref[...], preferred_element_type=F32)
    o_ref[...] = _layer_norm(DN_ALPHA * x + y, g_ref[...], b_ref[...])


def _ffn_layer(x2d, w1, w2, g, b, *, layer, tm):
    n = x2d.shape[0]
    row_spec = pl.BlockSpec((tm, D_MODEL), lambda i: (i, 0))

    def layer_spec(shape):
        return pl.BlockSpec((None,) + shape[1:], lambda i: (layer, 0, 0), pipeline_mode=pl.Buffered(1))

    return pl.pallas_call(
        _ffn_kernel,
        grid=(n // tm,),
        in_specs=[row_spec, layer_spec(w1.shape), layer_spec(w2.shape), _const_spec(g.shape), _const_spec(b.shape)],
        out_specs=row_spec,
        out_shape=jax.ShapeDtypeStruct(x2d.shape, F32),
        compiler_params=pltpu.CompilerParams(dimension_semantics=("arbitrary",), vmem_limit_bytes=VMEM_LIMIT),
        name="ffn",
    )(x2d, w1, w2, g, b)


def _cmlp_kernel(x_ref, win_ref, bin_ref, lng_ref, lnb_ref, ws_ref, bs_ref, wout_ref, g_ref, b_ref,
                 o_ref, *rest, blk, emit_v):
    if emit_v:
        v_ref, u_s, vpre_s, v_s, ub_s = rest
    else:
        u_s, vpre_s, v_s, ub_s = rest
    tm = x_ref.shape[0]
    xb = x_ref[...].astype(BF16)

    def dot_unit(dst, c_src, c_dst):
        dst[:, c_dst:c_dst + COL_BLK_CM] = jnp.dot(xb, win_ref[:, c_src:c_src + COL_BLK_CM],
                                                   preferred_element_type=F32)

    def gelu_unit(dst, c_src, c_dst):
        h = dst[:, c_dst:c_dst + COL_BLK_CM] + bin_ref[:, c_src:c_src + COL_BLK_CM]
        dst[:, c_dst:c_dst + COL_BLK_CM] = 0.5 * h * (
            1.0 + jnp.tanh(0.7978845608028654 * (h + 0.044715 * (h * h * h))))

    def norm_v():
        v = _layer_norm(vpre_s[...], lng_ref[...], lnb_ref[...])
        if emit_v:
            v_ref[...] = v
        v_s[...] = v.astype(BF16)

    def spatial_unit(g):
        ri = lax.broadcasted_iota(jnp.int32, (blk, blk), 0)
        ci = lax.broadcasted_iota(jnp.int32, (blk, blk), 1)
        gs = slice(g * CM_GROUP_W, (g + 1) * CM_GROUP_W)
        ws_g = jnp.where(ci <= ri, ws_ref[g, 0:blk, 0:blk], 0.0).astype(BF16)
        for i in range(tm // blk):
            rs = slice(i * blk, (i + 1) * blk)
            s = jnp.dot(ws_g, v_s[rs, gs], preferred_element_type=F32) + bs_ref[0:blk, gs]
            ub_s[rs, gs] = (u_s[rs, gs] * s).astype(BF16)

    def out_unit(c0):
        o_ref[:, c0:c0 + OUT_BLK] = jnp.dot(ub_s[...], wout_ref[:, c0:c0 + OUT_BLK],
                                            preferred_element_type=F32)

    def last():
        o_ref[...] = _layer_norm(DN_ALPHA * x_ref[...] + o_ref[...], g_ref[...], b_ref[...])

    n_blk = CM_WIDTH // COL_BLK_CM
    dots = ([functools.partial(dot_unit, vpre_s, CM_WIDTH + k * COL_BLK_CM, k * COL_BLK_CM) for k in range(n_blk)]
            + [functools.partial(dot_unit, u_s, k * COL_BLK_CM, k * COL_BLK_CM) for k in range(n_blk)])
    gelus = ([functools.partial(gelu_unit, vpre_s, CM_WIDTH + k * COL_BLK_CM, k * COL_BLK_CM) for k in range(n_blk)]
             + [functools.partial(gelu_unit, u_s, k * COL_BLK_CM, k * COL_BLK_CM) for k in range(n_blk)])
    spatial = [functools.partial(spatial_unit, g) for g in range(CM_GROUPS)]
    groups_per_blk = COL_BLK_CM // CM_GROUP_W
    order = [dots[0]]
    for k in range(2 * n_blk):
        order.extend(dots[k + 1:k + 2])
        order.append(gelus[k])
        if k == n_blk - 1:
            order.append(norm_v)
        if k >= n_blk:
            u_blk = k - n_blk
            order.extend(spatial[u_blk * groups_per_blk:(u_blk + 1) * groups_per_blk])
    order.extend([functools.partial(out_unit, c0) for c0 in range(0, D_MODEL, OUT_BLK)])
    order.append(last)
    for unit in order:
        unit()


def _cmlp_layer(x2d, w, *, tm, blk, emit_v):
    n = x2d.shape[0]
    row_spec = pl.BlockSpec((tm, D_MODEL), lambda i: (i, 0))
    consts = (w["w_in"], w["b_in"], w["ln_g"], w["ln_b"], w["w_s"], w["b_s"], w["w_out"], w["ln1_g"], w["ln1_b"])
    out_shape = [jax.ShapeDtypeStruct(x2d.shape, F32)]
    out_specs = [row_spec]
    if emit_v:
        out_shape.append(jax.ShapeDtypeStruct((n, CM_WIDTH), F32))
        out_specs.append(pl.BlockSpec((tm, CM_WIDTH), lambda i: (i, 0)))
    return pl.pallas_call(
        functools.partial(_cmlp_kernel, blk=blk, emit_v=emit_v),
        grid=(n // tm,),
        in_specs=[row_spec] + [_const_spec(c.shape) for c in consts],
        out_specs=out_specs,
        out_shape=out_shape,
        scratch_shapes=[pltpu.VMEM((tm, CM_WIDTH), F32),
                        pltpu.VMEM((tm, CM_WIDTH), F32),
                        pltpu.VMEM((tm, CM_WIDTH), BF16),
                        pltpu.VMEM((tm, CM_WIDTH), BF16)],
        compiler_params=pltpu.CompilerParams(dimension_semantics=("arbitrary",), vmem_limit_bytes=VMEM_LIMIT),
        name="cmlp_v" if emit_v else "cmlp",
    )(x2d, *consts)


def kernel(x_prompt, x_sample, state_ssm, state_conv, ssd_w_in, ssd_conv_w, ssd_conv_b, ssd_dt_bias, ssd_a_log, ssd_d, ssd_norm_w, ssd_w_out, cm_w_in, cm_b_in, cm_ln_g, cm_ln_b, cm_w_s, cm_b_s, cm_w_out, ffn_w1, ffn_w2, ln1_g, ln1_b, ln2_g, ln2_b):
    row = lambda a: a.reshape(1, -1)
    perm = jnp.asarray(HEAD_PERM)

    def head_lanes(a):
        return jnp.pad(a[perm], (0, LANES - SSD_HEADS)).reshape(1, LANES)

    w_in = ssd_w_in[0]
    zx_end = SSD_D_INNER + SSD_CONV_DIM
    ssd_w = {
        "w_zx": w_in.astype(BF16),
        "w_dt": jnp.pad(w_in[:, zx_end:][:, perm], ((0, 0), (0, LANES - SSD_HEADS))).astype(BF16),
        "conv_w": ssd_conv_w[0],
        "conv_b": row(ssd_conv_b[0]),
        "dt_bias": head_lanes(ssd_dt_bias[0]),
        "a_log": head_lanes(ssd_a_log[0]),
        "d_exp": row(jnp.repeat(ssd_d[0], SSD_HEAD_DIM)),
        "norm_w": row(ssd_norm_w[0]),
        "w_out": ssd_w_out[0].astype(BF16),
        "e2": jnp.asarray(_expand_matrix(), BF16),
        "ln_g": row(ln1_g[0]),
        "ln_b": row(ln1_b[0]),
    }
    cm_w = {
        "w_in": cm_w_in[0].astype(BF16),
        "b_in": row(cm_b_in[0]),
        "ln_g": row(cm_ln_g[0]),
        "ln_b": row(cm_ln_b[0]),
        "w_s": cm_w_s[0],
        "b_s": jnp.repeat(cm_b_s[0].T, CM_GROUP_W, axis=1),
        "w_out": cm_w_out[0].astype(BF16),
        "ln1_g": row(ln1_g[1]),
        "ln1_b": row(ln1_b[1]),
    }
    ffn_w1b, ffn_w2b = ffn_w1.astype(BF16), ffn_w2.astype(BF16)
    ffn = [functools.partial(_ffn_layer, w1=ffn_w1b, w2=ffn_w2b, g=row(ln2_g[i]), b=row(ln2_b[i]), layer=i, tm=512)
           for i in range(DEPTH)]

    def trunk(x, conv_prev, h0, *, sb_n, tt, blk, emit_v):
        shape = x.shape
        x1, conv_new, h_new = _ssd_layer(x, conv_prev, h0, ssd_w, sb_n=sb_n, tt=tt)
        x2 = ffn[0](x1.reshape(-1, D_MODEL))
        res = _cmlp_layer(x2, cm_w, tm=512, blk=blk, emit_v=emit_v)
        x4 = ffn[1](res[0])
        v_rows = res[1].reshape(shape[0], shape[1], CM_WIDTH) if emit_v else None
        return x4.reshape(shape), conv_new, h_new, v_rows

    y_p, conv_p, h_p, _ = trunk(x_prompt, None, None, sb_n=1, tt=256, blk=CM_BLOCK, emit_v=False)
    y_s, conv_s, h_s, v_s = trunk(x_sample, state_conv[0], state_ssm[0], sb_n=2, tt=CHUNK,
                                  blk=x_sample.shape[1], emit_v=True)
    return (y_p, y_s, h_p[None], conv_p[None], h_s[None], conv_s[None], v_s[None])
```

```python
import functools

import numpy as np
import jax
import jax.numpy as jnp
from jax import lax
from jax.experimental import pallas as pl
from jax.experimental.pallas import tpu as pltpu

F32 = jnp.float32
BF16 = jnp.bfloat16

D_MODEL = 1024
DEPTH = 2
CHUNK = 64
SSD_D_INNER = 2 * D_MODEL
SSD_HEAD_DIM = 64
SSD_HEADS = SSD_D_INNER // SSD_HEAD_DIM
SSD_GROUPS = 8
SSD_STATE = 128
SSD_CONV = 4
SSD_BC = SSD_GROUPS * SSD_STATE
SSD_CONV_DIM = SSD_D_INNER + 2 * SSD_BC
HEADS_PER_GROUP = SSD_HEADS // SSD_GROUPS
GROUP_W = HEADS_PER_GROUP * SSD_HEAD_DIM
NORM_GROUP_W = SSD_D_INNER // SSD_GROUPS
CM_BLOCK = 128
CM_WIDTH = 2 * D_MODEL
CM_GROUPS = 8
CM_GROUP_W = CM_WIDTH // CM_GROUPS
FFN_HIDDEN = 4 * D_MODEL
DN_ALPHA = (2 * DEPTH) ** 0.25
LN_EPS = 1e-5
LOG2_E = 1.4426950408889634
GELU_K = 0.7978845608028654
GELU_C = 0.044715

LANES = 128
CONV_PAD = 8
COL_BLK = 256
OUT_BLK = 256
COL_BLK_CM = 512
SCAN_UNIT_GROUPS = 4
VMEM_LIMIT = 56 * 1024 * 1024

HEAD_PERM = np.concatenate([np.arange(0, SSD_HEADS, 2), np.arange(1, SSD_HEADS, 2)])


def _expand_matrix():
    e = np.zeros((2 * LANES, SSD_D_INNER), np.float32)
    for k, h in enumerate(HEAD_PERM):
        e[k, h * SSD_HEAD_DIM:(h + 1) * SSD_HEAD_DIM] = 1.0
        e[LANES + k, h * SSD_HEAD_DIM:(h + 1) * SSD_HEAD_DIM] = 1.0
    return e


def _layer_norm(x, g, b):
    mu = jnp.mean(x, axis=-1, keepdims=True)
    xc = x - mu
    var = jnp.mean(xc * xc, axis=-1, keepdims=True)
    return xc * lax.rsqrt(var + LN_EPS) * g + b


def _silu(x):
    half = 0.5 * x
    return half + half * jnp.tanh(half)


def _split2(v):
    hi = v.astype(BF16)
    lo = (v - hi.astype(F32)).astype(BF16)
    return jnp.concatenate([hi, lo], axis=1)


def _const_spec(shape):
    nd = len(shape)
    return pl.BlockSpec(shape, lambda *_: (0,) * nd, pipeline_mode=pl.Buffered(1))


def _ssd_stage_a(x_ref, wz_ref, wdt_ref, cw_ref, cb_ref, ext_s, handoff, *, sb_n, tt):
    assert SSD_CONV == 4
    z_w, xs_w, bc_w, dtr_w = handoff
    rows = sb_n * tt
    hist = CONV_PAD - (SSD_CONV - 1)
    live = {}

    def first():
        live["xb"] = x_ref[...].reshape(rows, D_MODEL).astype(BF16)
        dtr_w[...] = jnp.dot(live["xb"], wdt_ref[...], preferred_element_type=F32)

    def dot_unit(c0):
        cs = slice(c0, c0 + COL_BLK)
        raw = jnp.dot(live["xb"], wz_ref[:, SSD_D_INNER + c0:SSD_D_INNER + c0 + COL_BLK],
                      preferred_element_type=F32)
        for sb in range(sb_n):
            ext_s[sb, CONV_PAD:CONV_PAD + tt, cs] = raw[sb * tt:(sb + 1) * tt]

    def conv_unit(c0):
        cs = slice(c0, c0 + COL_BLK)
        for sb in range(sb_n):
            ext = ext_s[sb, :, cs]
            back1 = pltpu.roll(ext, 1, axis=0)
            older = ext * cw_ref[1:2, cs] + back1 * cw_ref[0:1, cs]
            acc = (cb_ref[:, cs] + ext * cw_ref[3:4, cs] + back1 * cw_ref[2:3, cs]
                   + pltpu.roll(older, 2, axis=0))[CONV_PAD:]
            act = _silu(acc)
            rs = slice(sb * tt, (sb + 1) * tt)
            if c0 < SSD_D_INNER:
                xs_w[rs, cs] = act
            else:
                bc_w[rs, c0 - SSD_D_INNER:c0 - SSD_D_INNER + COL_BLK] = act.astype(BF16)
            ext_s[sb, hist:CONV_PAD, cs] = ext[tt + hist:tt + CONV_PAD]

    def z_unit(c0):
        z_w[:, c0:c0 + COL_BLK] = jnp.dot(live["xb"], wz_ref[:, c0:c0 + COL_BLK], preferred_element_type=F32)

    blocks = range(0, SSD_CONV_DIM, COL_BLK)
    return (first,
            [functools.partial(dot_unit, c0) for c0 in blocks],
            [functools.partial(conv_unit, c0) for c0 in blocks],
            [functools.partial(z_unit, c0) for c0 in range(0, SSD_D_INNER, COL_BLK)])


def _ssd_stage_b(handoff, dtb_ref, alog_ref, dexp_ref, nw_ref, e2_ref, yb_w,
                 dtx_s, acx_s, y_s, ac_s, h_s, *, sb_n, tt):
    z_r, xs_r, bc_r, dtr_r = handoff
    rows = sb_n * tt
    nch = tt // CHUNK
    n_pairs = SSD_HEADS // 2
    live = {}

    def prologue():
        v = dtr_r[...] + dtb_ref[...]
        dt = jnp.maximum(v, 0.0) + jnp.log1p(jnp.exp(-jnp.abs(v)))
        a_neg = -jnp.exp(alog_ref[...])
        d_a = dt * a_neg
        ri = lax.broadcasted_iota(jnp.int32, (rows, rows), 0)
        ci = lax.broadcasted_iota(jnp.int32, (rows, rows), 1)
        same_chunk = (ri & -CHUNK) == (ci & -CHUNK)
        tril = jnp.where((ci <= ri) & same_chunk, 1.0, 0.0).astype(BF16)
        cs = jnp.dot(tril, _split2(d_a), preferred_element_type=F32)
        acum = (cs[:, 0:LANES] + cs[:, LANES:]) * LOG2_E
        ac_s[...] = acum
        acx_s[...] = jnp.dot(_split2(acum), e2_ref[...], preferred_element_type=F32)
        dtx_s[...] = xs_r[...] * jnp.dot(_split2(dt), e2_ref[...], preferred_element_type=F32)
        row_i = lax.broadcasted_iota(jnp.int32, (CHUNK, LANES), 0)
        lane_i = lax.broadcasted_iota(jnp.int32, (CHUNK, LANES), 1)
        live["causal2"] = (lane_i & (CHUNK - 1)) <= row_i
        live["first_half"] = lane_i < CHUNK

    def scan_unit(chunk, g0):
        sb = chunk // nch
        r0 = chunk * CHUNK
        rs = slice(r0, r0 + CHUNK)
        if g0 == 0:
            ac_t = ac_s[rs, :].T
            live["pair_rows"] = jnp.concatenate([ac_t[0:n_pairs], ac_t[n_pairs:2 * n_pairs]], axis=1)
        pair_rows, causal2, first_half = live["pair_rows"], live["causal2"], live["first_half"]
        groups = range(g0, g0 + SCAN_UNIT_GROUPS)
        b, c, h, acx, cb2, y_off = {}, {}, {}, {}, {}, {}
        for g in groups:
            gs = slice(g * GROUP_W, (g + 1) * GROUP_W)
            b[g] = bc_r[rs, g * SSD_STATE:(g + 1) * SSD_STATE]
            c[g] = bc_r[rs, SSD_BC + g * SSD_STATE:SSD_BC + (g + 1) * SSD_STATE]
            bb = jnp.concatenate([b[g], b[g]], axis=0)
            cb2[g] = lax.dot_general(c[g], bb, (((1,), (1,)), ((), ())), preferred_element_type=F32)
            h[g] = h_s[sb, g]
            acx[g] = acx_s[rs, gs]
            y_off[g] = jnp.dot(c[g], h[g].astype(BF16), preferred_element_type=F32)
        y_diag = {}
        for g in groups:
            for jj in range(2):
                j = 2 * g + jj
                ls = slice(j * LANES, (j + 1) * LANES)
                seg = acx_s[rs, ls] - pair_rows[j:j + 1, :]
                decay = jnp.where(causal2, jnp.exp2(seg), 0.0)
                m2 = (cb2[g] * decay).astype(BF16)
                xp = dtx_s[rs, ls]
                xbd = jnp.concatenate([jnp.where(first_half, xp, 0.0), jnp.where(first_half, 0.0, xp)], axis=0)
                y_diag[j] = jnp.dot(m2, xbd.astype(BF16), preferred_element_type=F32)
        upd, a_last = {}, {}
        for g in groups:
            gs = slice(g * GROUP_W, (g + 1) * GROUP_W)
            a_last[g] = acx_s[r0 + CHUNK - 1:r0 + CHUNK, gs]
            xw = (dtx_s[rs, gs] * jnp.exp2(a_last[g] - acx[g])).astype(BF16)
            upd[g] = lax.dot_general(b[g], xw, (((0,), (0,)), ((), ())), preferred_element_type=F32)
        for g in groups:
            y_g = jnp.concatenate([y_diag[2 * g], y_diag[2 * g + 1]], axis=1) + y_off[g] * jnp.exp2(acx[g])
            y_s[rs, g * GROUP_W:(g + 1) * GROUP_W] = y_g
            h_s[sb, g] = h[g] * jnp.exp2(a_last[g]) + upd[g]

    def gate_unit(g):
        gs = slice(g * NORM_GROUP_W, (g + 1) * NORM_GROUP_W)
        y = y_s[:, gs] + dexp_ref[:, gs] * xs_r[:, gs]
        zg = z_r[:, gs]
        gg = y * _silu(zg)
        ms = jnp.mean(gg * gg, axis=-1, keepdims=True)
        yb_w[:, gs] = (gg * lax.rsqrt(ms + LN_EPS) * nw_ref[:, gs]).astype(BF16)

    return (prologue,
            [functools.partial(scan_unit, c, g) for c in range(sb_n * nch)
             for g in range(0, SSD_GROUPS, SCAN_UNIT_GROUPS)],
            [functools.partial(gate_unit, g) for g in range(SSD_GROUPS)])


def _ssd_stage_c(yb_r, x_ref, wout_ref, g_ref, b_ref, o_ref, *, sb_n, tt):
    rows = sb_n * tt

    def out_unit(c0):
        mix = jnp.dot(yb_r[...], wout_ref[:, c0:c0 + OUT_BLK], preferred_element_type=F32)
        o_ref[:, :, c0:c0 + OUT_BLK] = mix.reshape(sb_n, tt, OUT_BLK)

    def last():
        mix = o_ref[...].reshape(rows, D_MODEL)
        x = x_ref[...].reshape(rows, D_MODEL)
        o_ref[...] = _layer_norm(DN_ALPHA * x + mix, g_ref[...], b_ref[...]).reshape(sb_n, tt, D_MODEL)

    return [functools.partial(out_unit, c0) for c0 in range(0, D_MODEL, OUT_BLK)], last


def _ssd_trace_order(stage_a, stage_b, stage_c, state_units=()):
    a_first, a_dots, a_convs, a_z = stage_a
    b_pro, b_scans, b_gates = stage_b
    c_units = list(stage_c[0]) + [stage_c[1]]
    lead = 4
    n_blk = len(a_dots)
    order = [a_first] + list(state_units[:1]) + [a_dots[0], a_dots[1], b_pro, a_dots[2], a_dots[3]]
    nxt_conv, nxt_dot = 0, lead

    def conv_and_dot():
        nonlocal nxt_conv, nxt_dot
        order.append(a_convs[nxt_conv])
        nxt_conv += 1
        if nxt_dot < n_blk:
            order.append(a_dots[nxt_dot])
            nxt_dot += 1

    n_scan = len(b_scans)
    n_beside_scans = 10 * n_scan * SCAN_UNIT_GROUPS // 32
    n_c = len(c_units)
    for u, scan in enumerate(b_scans):
        order.append(scan)
        for _ in range((u + 1) * n_beside_scans // n_scan - u * n_beside_scans // n_scan):
            conv_and_dot()
        for _ in range((u + 1) * n_c // n_scan - u * n_c // n_scan):
            order.append(c_units.pop(0))
    for g, gate in enumerate(b_gates):
        order.append(gate)
        order.extend(a_z[g:g + 1])
        if nxt_conv < n_blk:
            conv_and_dot()
    order.extend(a_z[len(b_gates):])
    while nxt_conv < n_blk:
        conv_and_dot()
    order.extend(c_units)
    n_units = (1 + 2 * n_blk + len(a_z) + 1 + len(b_scans) + len(b_gates) + len(stage_c[0]) + 1
               + len(state_units))
    assert len(order) == n_units and len(set(map(id, order))) == n_units
    return order


def _ssd_kernel(*refs, sb_n, tt, nt, n_tiles, has_state):
    if has_state:
        (x_ref, xres_ref, cprev_ref, h0_ref, *refs) = refs
    else:
        (x_ref, xres_ref, *refs) = refs
    (wz_ref, wdt_ref, cw_ref, cb_ref, dtb_ref, alog_ref, dexp_ref, nw_ref, wout_ref, e2_ref,
     g_ref, b_ref, o_ref, cnew_ref, hout_ref,
     ext_s, dtx_s, acx_s, y_s, ac_s, h_s, yb0_s, yb1_s, *handoffs) = refs
    n_hand = len(handoffs) // 2
    hand = (tuple(handoffs[:n_hand]), tuple(handoffs[n_hand:]))
    yb = (yb0_s, yb1_s)
    i = pl.program_id(0)
    t_a = jnp.minimum(i, n_tiles - 1) % nt
    t_b = jnp.clip(i - 1, 0, n_tiles - 1) % nt
    hist = CONV_PAD - (SSD_CONV - 1)

    @pl.when(i == 0)
    def _zero_first_handoff():
        for ref in hand[1] + (yb[0],):
            ref[...] = jnp.zeros(ref.shape, ref.dtype)

    @pl.when(t_a == 0)
    def _conv_history():
        ext_s[:, 0:CONV_PAD, :] = jnp.zeros((sb_n, CONV_PAD, SSD_CONV_DIM), F32)
        if has_state:
            for sb in range(sb_n):
                ext_s[sb, hist:CONV_PAD, :] = cprev_ref[sb]

    def state_in():
        if has_state:
            for sb in range(sb_n):
                for g in range(SSD_GROUPS):
                    hg = h0_ref[sb, g * HEADS_PER_GROUP:(g + 1) * HEADS_PER_GROUP]
                    h_s[sb, g] = hg.reshape(GROUP_W, SSD_STATE).T
        else:
            h_s[...] = jnp.zeros(h_s.shape, F32)

    inline_state_in = has_state and nt == 1
    state_units = (state_in,) if inline_state_in else ()
    if not inline_state_in:
        pl.when(t_b == 0)(state_in)

    def all_stages(parity):
        stage_a = _ssd_stage_a(x_ref, wz_ref, wdt_ref, cw_ref, cb_ref, ext_s, hand[parity], sb_n=sb_n, tt=tt)
        stage_b = _ssd_stage_b(hand[1 - parity], dtb_ref, alog_ref, dexp_ref, nw_ref, e2_ref, yb[1 - parity],
                               dtx_s, acx_s, y_s, ac_s, h_s, sb_n=sb_n, tt=tt)
        stage_c = _ssd_stage_c(yb[parity], xres_ref, wout_ref, g_ref, b_ref, o_ref, sb_n=sb_n, tt=tt)
        for unit in _ssd_trace_order(stage_a, stage_b, stage_c, state_units):
            unit()

    @pl.when(i % 2 == 0)
    def _even():
        all_stages(0)

    @pl.when(i % 2 == 1)
    def _odd():
        all_stages(1)

    @pl.when((t_a == nt - 1) & (i < n_tiles))
    def _conv_out():
        cnew_ref[...] = ext_s[:, hist:CONV_PAD, :]

    @pl.when((t_b == nt - 1) & (i >= 1) & (i <= n_tiles))
    def _state_out():
        for sb in range(sb_n):
            for g in range(SSD_GROUPS):
                hout_ref[sb, g * HEADS_PER_GROUP:(g + 1) * HEADS_PER_GROUP] = (
                    h_s[sb, g].T.reshape(HEADS_PER_GROUP, SSD_HEAD_DIM, SSD_STATE))


def _ssd_layer(x, conv_prev, h0, w, *, sb_n, tt):
    nseq, seq_len, _ = x.shape
    has_state = h0 is not None
    rows = sb_n * tt
    nt = seq_len // tt
    n_tiles = (nseq // sb_n) * nt

    def tile_a(i):
        return jnp.minimum(i, n_tiles - 1)

    def tile_b(i):
        return jnp.clip(i - 1, 0, n_tiles - 1)

    def tile_c(i):
        return jnp.maximum(i - 2, 0)

    x_spec = pl.BlockSpec((sb_n, tt, D_MODEL), lambda i: (tile_a(i) // nt, tile_a(i) % nt, 0))
    o_spec = pl.BlockSpec((sb_n, tt, D_MODEL), lambda i: (tile_c(i) // nt, tile_c(i) % nt, 0))
    conv_spec = pl.BlockSpec((sb_n, SSD_CONV - 1, SSD_CONV_DIM), lambda i: (tile_a(i) // nt, 0, 0))
    state_spec = pl.BlockSpec((sb_n, SSD_HEADS, SSD_HEAD_DIM, SSD_STATE), lambda i: (tile_b(i) // nt, 0, 0, 0))
    consts = (w["w_zx"], w["w_dt"], w["conv_w"], w["conv_b"], w["dt_bias"], w["a_log"], w["d_exp"],
              w["norm_w"], w["w_out"], w["e2"], w["ln_g"], w["ln_b"])
    in_specs = ([x_spec, o_spec] + ([conv_spec, state_spec] if has_state else [])
                + [_const_spec(c.shape) for c in consts])
    args = (x, x) + ((conv_prev, h0) if has_state else ()) + consts
    handoff = [
        pltpu.VMEM((rows, SSD_D_INNER), F32),
        pltpu.VMEM((rows, SSD_D_INNER), F32),
        pltpu.VMEM((rows, 2 * SSD_BC), BF16),
        pltpu.VMEM((rows, LANES), F32),
    ]
    scratch = [
        pltpu.VMEM((sb_n, tt + CONV_PAD, SSD_CONV_DIM), F32),
        pltpu.VMEM((rows, SSD_D_INNER), F32),
        pltpu.VMEM((rows, SSD_D_INNER), F32),
        pltpu.VMEM((rows, SSD_D_INNER), F32),
        pltpu.VMEM((rows, LANES), F32),
        pltpu.VMEM((sb_n, SSD_GROUPS, SSD_STATE, GROUP_W), F32),
        pltpu.VMEM((rows, SSD_D_INNER), BF16),
        pltpu.VMEM((rows, SSD_D_INNER), BF16),
    ] + handoff + handoff
    out_shape = (
        jax.ShapeDtypeStruct(x.shape, F32),
        jax.ShapeDtypeStruct((nseq, SSD_CONV - 1, SSD_CONV_DIM), F32),
        jax.ShapeDtypeStruct((nseq, SSD_HEADS, SSD_HEAD_DIM, SSD_STATE), F32),
    )
    return pl.pallas_call(
        functools.partial(_ssd_kernel, sb_n=sb_n, tt=tt, nt=nt, n_tiles=n_tiles, has_state=has_state),
        grid=(n_tiles + 2,),
        in_specs=in_specs,
        out_specs=(o_spec, conv_spec, state_spec),
        out_shape=out_shape,
        scratch_shapes=scratch,
        compiler_params=pltpu.CompilerParams(dimension_semantics=("arbitrary",), vmem_limit_bytes=VMEM_LIMIT),
        name="ssd_state" if has_state else "ssd_fresh",
    )(*args)


def _ffn_kernel(x_ref, w1_ref, w2_ref, g_ref, b_ref, o_ref, sum_s, *, n_tiles):
    i = pl.program_id(0)

    def mlp():
        x = x_ref[...]
        h = jnp.dot(x.astype(BF16), w1_ref[...], preferred_element_type=F32)
        h = jnp.maximum(h, 0.0)
        y = jnp.dot((h * h).astype(BF16), w2_ref[...], preferred_element_type=F32)
        sum_s[...] = DN_ALPHA * x + y

    def norm():
        o_ref[...] = _layer_norm(sum_s[...], g_ref[...], b_ref[...])

    @pl.when(i == 0)
    def _first():
        mlp()

    @pl.when((i > 0) & (i < n_tiles))
    def _steady():
        norm()
        mlp()

    @pl.when(i == n_tiles)
    def _last():
        norm()


def _ffn_layer(x2d, w1, w2, g, b, *, layer, tm):
    n_tiles = x2d.shape[0] // tm
    x_spec = pl.BlockSpec((tm, D_MODEL), lambda i: (jnp.minimum(i, n_tiles - 1), 0))
    o_spec = pl.BlockSpec((tm, D_MODEL), lambda i: (jnp.maximum(i - 1, 0), 0))

    def layer_spec(shape):
        return pl.BlockSpec((None,) + shape[1:], lambda i: (layer, 0, 0), pipeline_mode=pl.Buffered(1))

    return pl.pallas_call(
        functools.partial(_ffn_kernel, n_tiles=n_tiles),
        grid=(n_tiles + 1,),
        in_specs=[x_spec, layer_spec(w1.shape), layer_spec(w2.shape), _const_spec(g.shape), _const_spec(b.shape)],
        out_specs=o_spec,
        out_shape=jax.ShapeDtypeStruct(x2d.shape, F32),
        scratch_shapes=[pltpu.VMEM((tm, D_MODEL), F32)],
        compiler_params=pltpu.CompilerParams(dimension_semantics=("arbitrary",), vmem_limit_bytes=VMEM_LIMIT),
        name="ffn",
    )(x2d, w1, w2, g, b)


def _cmlp_kernel(x_ref, win_ref, bin_ref, lng_ref, lnb_ref, ws_ref, bs_ref, wout_ref, g_ref, b_ref,
                 o_ref, *rest, blk, emit_v):
    if emit_v:
        v_ref, u_s, vpre_s, v_s, ub_s = rest
    else:
        u_s, vpre_s, v_s, ub_s = rest
    tm = x_ref.shape[0]
    xb = x_ref[...].astype(BF16)

    def dot_unit(dst, c_src, c_dst):
        dst[:, c_dst:c_dst + COL_BLK_CM] = jnp.dot(xb, win_ref[:, c_src:c_src + COL_BLK_CM],
                                                   preferred_element_type=F32)

    def gelu_unit(dst, c_src, c_dst):
        h = dst[:, c_dst:c_dst + COL_BLK_CM] + bin_ref[:, c_src:c_src + COL_BLK_CM]
        half = 0.5 * h
        inner = h * (GELU_K + (GELU_K * GELU_C) * (h * h))
        dst[:, c_dst:c_dst + COL_BLK_CM] = half + half * jnp.tanh(inner)

    def norm_v():
        v = _layer_norm(vpre_s[...], lng_ref[...], lnb_ref[...])
        if emit_v:
            v_ref[...] = v
        v_s[...] = v.astype(BF16)

    def spatial_unit(g):
        ri = lax.broadcasted_iota(jnp.int32, (blk, blk), 0)
        ci = lax.broadcasted_iota(jnp.int32, (blk, blk), 1)
        gs = slice(g * CM_GROUP_W, (g + 1) * CM_GROUP_W)
        ws_g = jnp.where(ci <= ri, ws_ref[g, 0:blk, 0:blk], 0.0).astype(BF16)
        for i in range(tm // blk):
            rs = slice(i * blk, (i + 1) * blk)
            s = jnp.dot(ws_g, v_s[rs, gs], preferred_element_type=F32) + bs_ref[0:blk, gs]
            ub_s[rs, gs] = (u_s[rs, gs] * s).astype(BF16)

    def out_unit(c0):
        o_ref[:, c0:c0 + OUT_BLK] = jnp.dot(ub_s[...], wout_ref[:, c0:c0 + OUT_BLK],
                                            preferred_element_type=F32)

    def last():
        o_ref[...] = _layer_norm(DN_ALPHA * x_ref[...] + o_ref[...], g_ref[...], b_ref[...])

    n_blk = CM_WIDTH // COL_BLK_CM
    dots = ([functools.partial(dot_unit, vpre_s, CM_WIDTH + k * COL_BLK_CM, k * COL_BLK_CM) for k in range(n_blk)]
            + [functools.partial(dot_unit, u_s, k * COL_BLK_CM, k * COL_BLK_CM) for k in range(n_blk)])
    gelus = ([functools.partial(gelu_unit, vpre_s, CM_WIDTH + k * COL_BLK_CM, k * COL_BLK_CM) for k in range(n_blk)]
             + [functools.partial(gelu_unit, u_s, k * COL_BLK_CM, k * COL_BLK_CM) for k in range(n_blk)])
    spatial = [functools.partial(spatial_unit, g) for g in range(CM_GROUPS)]
    groups_per_blk = COL_BLK_CM // CM_GROUP_W
    order = [dots[0]]
    for k in range(2 * n_blk):
        order.extend(dots[k + 1:k + 2])
        order.append(gelus[k])
        if k == n_blk - 1:
            order.append(norm_v)
        if k >= n_blk:
            u_blk = k - n_blk
            order.extend(spatial[u_blk * groups_per_blk:(u_blk + 1) * groups_per_blk])
    order.extend([functools.partial(out_unit, c0) for c0 in range(0, D_MODEL, OUT_BLK)])
    order.append(last)
    for unit in order:
        unit()


def _cmlp_layer(x2d, w, *, tm, blk, emit_v):
    n = x2d.shape[0]
    row_spec = pl.BlockSpec((tm, D_MODEL), lambda i: (i, 0))
    consts = (w["w_in"], w["b_in"], w["ln_g"], w["ln_b"], w["w_s"], w["b_s"], w["w_out"], w["ln1_g"], w["ln1_b"])
    out_shape = [jax.ShapeDtypeStruct(x2d.shape, F32)]
    out_specs = [row_spec]
    if emit_v:
        out_shape.append(jax.ShapeDtypeStruct((n, CM_WIDTH), F32))
        out_specs.append(pl.BlockSpec((tm, CM_WIDTH), lambda i: (i, 0)))
    return pl.pallas_call(
        functools.partial(_cmlp_kernel, blk=blk, emit_v=emit_v),
        grid=(n // tm,),
        in_specs=[row_spec] + [_const_spec(c.shape) for c in consts],
        out_specs=out_specs,
        out_shape=out_shape,
        scratch_shapes=[pltpu.VMEM((tm, CM_WIDTH), F32),
                        pltpu.VMEM((tm, CM_WIDTH), F32),
                        pltpu.VMEM((tm, CM_WIDTH), BF16),
                        pltpu.VMEM((tm, CM_WIDTH), BF16)],
        compiler_params=pltpu.CompilerParams(dimension_semantics=("arbitrary",), vmem_limit_bytes=VMEM_LIMIT),
        name="cmlp_v" if emit_v else "cmlp",
    )(x2d, *consts)


def kernel(x_prompt, x_sample, state_ssm, state_conv, ssd_w_in, ssd_conv_w, ssd_conv_b, ssd_dt_bias, ssd_a_log, ssd_d, ssd_norm_w, ssd_w_out, cm_w_in, cm_b_in, cm_ln_g, cm_ln_b, cm_w_s, cm_b_s, cm_w_out, ffn_w1, ffn_w2, ln1_g, ln1_b, ln2_g, ln2_b):
    row = lambda a: a.reshape(1, -1)
    perm = jnp.asarray(HEAD_PERM)

    def head_lanes(a):
        return jnp.pad(a[perm], (0, LANES - SSD_HEADS)).reshape(1, LANES)

    w_in = ssd_w_in[0]
    zx_end = SSD_D_INNER + SSD_CONV_DIM
    ssd_w = {
        "w_zx": w_in.astype(BF16),
        "w_dt": jnp.pad(w_in[:, zx_end:][:, perm], ((0, 0), (0, LANES - SSD_HEADS))).astype(BF16),
        "conv_w": ssd_conv_w[0],
        "conv_b": row(ssd_conv_b[0]),
        "dt_bias": head_lanes(ssd_dt_bias[0]),
        "a_log": head_lanes(ssd_a_log[0]),
        "d_exp": row(jnp.repeat(ssd_d[0], SSD_HEAD_DIM)),
        "norm_w": row(ssd_norm_w[0]),
        "w_out": ssd_w_out[0].astype(BF16),
        "e2": jnp.asarray(_expand_matrix(), BF16),
        "ln_g": row(ln1_g[0]),
        "ln_b": row(ln1_b[0]),
    }
    cm_w = {
        "w_in": cm_w_in[0].astype(BF16),
        "b_in": row(cm_b_in[0]),
        "ln_g": row(cm_ln_g[0]),
        "ln_b": row(cm_ln_b[0]),
        "w_s": cm_w_s[0],
        "b_s": jnp.repeat(cm_b_s[0].T, CM_GROUP_W, axis=1),
        "w_out": cm_w_out[0].astype(BF16),
        "ln1_g": row(ln1_g[1]),
        "ln1_b": row(ln1_b[1]),
    }
    ffn_w1b, ffn_w2b = ffn_w1.astype(BF16), ffn_w2.astype(BF16)
    ffn = [functools.partial(_ffn_layer, w1=ffn_w1b, w2=ffn_w2b, g=row(ln2_g[i]), b=row(ln2_b[i]), layer=i, tm=512)
           for i in range(DEPTH)]

    def trunk(x, conv_prev, h0, *, sb_n, tt, blk, emit_v):
        shape = x.shape
        x1, conv_new, h_new = _ssd_layer(x, conv_prev, h0, ssd_w, sb_n=sb_n, tt=tt)
        x2 = ffn[0](x1.reshape(-1, D_MODEL))
        res = _cmlp_layer(x2, cm_w, tm=512, blk=blk, emit_v=emit_v)
        x4 = ffn[1](res[0])
        v_rows = res[1].reshape(shape[0], shape[1], CM_WIDTH) if emit_v else None
        return x4.reshape(shape), conv_new, h_new, v_rows

    y_p, conv_p, h_p, _ = trunk(x_prompt, None, None, sb_n=1, tt=256, blk=CM_BLOCK, emit_v=False)
    y_s, conv_s, h_s, v_s = trunk(x_sample, state_conv[0], state_ssm[0], sb_n=2, tt=CHUNK,
                                  blk=x_sample.shape[1], emit_v=True)
    return (y_p, y_s, h_p[None], conv_p[None], h_s[None], conv_s[None], v_s[None])
```

```python
import functools

import numpy as np
import jax
import jax.numpy as jnp
from jax import lax
from jax.experimental import pallas as pl
from jax.experimental.pallas import tpu as pltpu

F32 = jnp.float32
BF16 = jnp.bfloat16

D_MODEL = 1024
DEPTH = 2
CHUNK = 64
SSD_D_INNER = 2 * D_MODEL
SSD_HEAD_DIM = 64
SSD_HEADS = SSD_D_INNER // SSD_HEAD_DIM
SSD_GROUPS = 8
SSD_STATE = 128
SSD_CONV = 4
SSD_BC = SSD_GROUPS * SSD_STATE
SSD_CONV_DIM = SSD_D_INNER + 2 * SSD_BC
HEADS_PER_GROUP = SSD_HEADS // SSD_GROUPS
GROUP_W = HEADS_PER_GROUP * SSD_HEAD_DIM
NORM_GROUP_W = SSD_D_INNER // SSD_GROUPS
CM_BLOCK = 128
CM_WIDTH = 2 * D_MODEL
CM_GROUPS = 8
CM_GROUP_W = CM_WIDTH // CM_GROUPS
FFN_HIDDEN = 4 * D_MODEL
DN_ALPHA = (2 * DEPTH) ** 0.25
LN_EPS = 1e-5
LOG2_E = 1.4426950408889634
GELU_K = 0.7978845608028654
GELU_C = 0.044715

LANES = 128
CONV_PAD = 8
COL_BLK = 256
OUT_BLK = 256
COL_BLK_CM = 512
SCAN_UNIT_GROUPS = 4
VMEM_LIMIT = 56 * 1024 * 1024

HEAD_PERM = np.concatenate([np.arange(0, SSD_HEADS, 2), np.arange(1, SSD_HEADS, 2)])


def _expand_matrix():
    e = np.zeros((2 * LANES, SSD_D_INNER), np.float32)
    for k, h in enumerate(HEAD_PERM):
        e[k, h * SSD_HEAD_DIM:(h + 1) * SSD_HEAD_DIM] = 1.0
        e[LANES + k, h * SSD_HEAD_DIM:(h + 1) * SSD_HEAD_DIM] = 1.0
    return e


def _layer_norm(x, g, b):
    mu = jnp.mean(x, axis=-1, keepdims=True)
    xc = x - mu
    var = jnp.mean(xc * xc, axis=-1, keepdims=True)
    return xc * lax.rsqrt(var + LN_EPS) * g + b


def _silu(x):
    half = 0.5 * x
    return half + half * jnp.tanh(half)


def _split2(v):
    hi = v.astype(BF16)
    lo = (v - hi.astype(F32)).astype(BF16)
    return jnp.concatenate([hi, lo], axis=1)


def _const_spec(shape):
    nd = len(shape)
    return pl.BlockSpec(shape, lambda *_: (0,) * nd, pipeline_mode=pl.Buffered(1))


def _ssd_stage_a(x_ref, wz_ref, wdt_ref, cw_ref, cb_ref, ext_s, handoff, *, sb_n, tt):
    assert SSD_CONV == 4
    z_w, xs_w, bc_w, dtr_w = handoff
    rows = sb_n * tt
    hist = CONV_PAD - (SSD_CONV - 1)
    live = {}

    def first():
        live["xb"] = x_ref[...].reshape(rows, D_MODEL).astype(BF16)
        dtr_w[...] = jnp.dot(live["xb"], wdt_ref[...], preferred_element_type=F32)

    def dot_unit(c0):
        cs = slice(c0, c0 + COL_BLK)
        raw = jnp.dot(live["xb"], wz_ref[:, SSD_D_INNER + c0:SSD_D_INNER + c0 + COL_BLK],
                      preferred_element_type=F32)
        for sb in range(sb_n):
            ext_s[sb, CONV_PAD:CONV_PAD + tt, cs] = raw[sb * tt:(sb + 1) * tt]

    def conv_unit(c0):
        cs = slice(c0, c0 + COL_BLK)
        for sb in range(sb_n):
            ext = ext_s[sb, :, cs]
            back1 = pltpu.roll(ext, 1, axis=0)
            older = ext * cw_ref[1:2, cs] + back1 * cw_ref[0:1, cs]
            acc = (cb_ref[:, cs] + ext * cw_ref[3:4, cs] + back1 * cw_ref[2:3, cs]
                   + pltpu.roll(older, 2, axis=0))[CONV_PAD:]
            act = _silu(acc)
            rs = slice(sb * tt, (sb + 1) * tt)
            if c0 < SSD_D_INNER:
                xs_w[rs, cs] = act
            else:
                bc_w[rs, c0 - SSD_D_INNER:c0 - SSD_D_INNER + COL_BLK] = act.astype(BF16)
            ext_s[sb, hist:CONV_PAD, cs] = ext[tt + hist:tt + CONV_PAD]

    def z_unit(c0):
        z_w[:, c0:c0 + COL_BLK] = jnp.dot(live["xb"], wz_ref[:, c0:c0 + COL_BLK], preferred_element_type=F32)

    blocks = range(0, SSD_CONV_DIM, COL_BLK)
    return (first,
            [functools.partial(dot_unit, c0) for c0 in blocks],
            [functools.partial(conv_unit, c0) for c0 in blocks],
            [functools.partial(z_unit, c0) for c0 in range(0, SSD_D_INNER, COL_BLK)])


def _ssd_stage_b(handoff, dtb_ref, alog_ref, dexp_ref, nw_ref, e2_ref, yb_w,
                 dtx_s, acx_s, y_s, ac_s, h_s, *, sb_n, tt):
    z_r, xs_r, bc_r, dtr_r = handoff
    rows = sb_n * tt
    nch = tt // CHUNK
    n_pairs = SSD_HEADS // 2
    live = {}

    def prologue():
        v = dtr_r[...] + dtb_ref[...]
        dt = jnp.maximum(v, 0.0) + jnp.log1p(jnp.exp(-jnp.abs(v)))
        a_neg = -jnp.exp(alog_ref[...])
        d_a = dt * a_neg
        ri = lax.broadcasted_iota(jnp.int32, (rows, rows), 0)
        ci = lax.broadcasted_iota(jnp.int32, (rows, rows), 1)
        same_chunk = (ri & -CHUNK) == (ci & -CHUNK)
        tril = jnp.where((ci <= ri) & same_chunk, 1.0, 0.0).astype(BF16)
        cs = jnp.dot(tril, _split2(d_a), preferred_element_type=F32)
        acum = (cs[:, 0:LANES] + cs[:, LANES:]) * LOG2_E
        ac_s[...] = acum
        acx_s[...] = jnp.dot(_split2(acum), e2_ref[...], preferred_element_type=F32)
        dtx_s[...] = xs_r[...] * jnp.dot(_split2(dt), e2_ref[...], preferred_element_type=F32)
        row_i = lax.broadcasted_iota(jnp.int32, (CHUNK, LANES), 0)
        lane_i = lax.broadcasted_iota(jnp.int32, (CHUNK, LANES), 1)
        live["causal2"] = (lane_i & (CHUNK - 1)) <= row_i
        live["first_half"] = lane_i < CHUNK

    def scan_unit(chunk, g0):
        sb = chunk // nch
        r0 = chunk * CHUNK
        rs = slice(r0, r0 + CHUNK)
        if g0 == 0:
            ac_t = ac_s[rs, :].T
            live["pair_rows"] = jnp.concatenate([ac_t[0:n_pairs], ac_t[n_pairs:2 * n_pairs]], axis=1)
        pair_rows, causal2, first_half = live["pair_rows"], live["causal2"], live["first_half"]
        groups = range(g0, g0 + SCAN_UNIT_GROUPS)
        b, c, h, acx, cb2, y_off = {}, {}, {}, {}, {}, {}
        for g in groups:
            gs = slice(g * GROUP_W, (g + 1) * GROUP_W)
            b[g] = bc_r[rs, g * SSD_STATE:(g + 1) * SSD_STATE]
            c[g] = bc_r[rs, SSD_BC + g * SSD_STATE:SSD_BC + (g + 1) * SSD_STATE]
            bb = jnp.concatenate([b[g], b[g]], axis=0)
            cb2[g] = lax.dot_general(c[g], bb, (((1,), (1,)), ((), ())), preferred_element_type=F32)
            h[g] = h_s[sb, g]
            acx[g] = acx_s[rs, gs]
            y_off[g] = jnp.dot(c[g], h[g].astype(BF16), preferred_element_type=F32)
        y_diag = {}
        for g in groups:
            for jj in range(2):
                j = 2 * g + jj
                ls = slice(j * LANES, (j + 1) * LANES)
                seg = acx_s[rs, ls] - pair_rows[j:j + 1, :]
                decay = jnp.where(causal2, jnp.exp2(seg), 0.0)
                m2 = (cb2[g] * decay).astype(BF16)
                xp = dtx_s[rs, ls]
                xbd = jnp.concatenate([jnp.where(first_half, xp, 0.0), jnp.where(first_half, 0.0, xp)], axis=0)
                y_diag[j] = jnp.dot(m2, xbd.astype(BF16), preferred_element_type=F32)
        upd, a_last = {}, {}
        for g in groups:
            gs = slice(g * GROUP_W, (g + 1) * GROUP_W)
            a_last[g] = acx_s[r0 + CHUNK - 1:r0 + CHUNK, gs]
            xw = (dtx_s[rs, gs] * jnp.exp2(a_last[g] - acx[g])).astype(BF16)
            upd[g] = lax.dot_general(b[g], xw, (((0,), (0,)), ((), ())), preferred_element_type=F32)
        for g in groups:
            y_g = jnp.concatenate([y_diag[2 * g], y_diag[2 * g + 1]], axis=1) + y_off[g] * jnp.exp2(acx[g])
            y_s[rs, g * GROUP_W:(g + 1) * GROUP_W] = y_g
            h_s[sb, g] = h[g] * jnp.exp2(a_last[g]) + upd[g]

    def gate_unit(g):
        gs = slice(g * NORM_GROUP_W, (g + 1) * NORM_GROUP_W)
        y = y_s[:, gs] + dexp_ref[:, gs] * xs_r[:, gs]
        zg = z_r[:, gs]
        gg = y * _silu(zg)
        ms = jnp.mean(gg * gg, axis=-1, keepdims=True)
        yb_w[:, gs] = (gg * lax.rsqrt(ms + LN_EPS) * nw_ref[:, gs]).astype(BF16)

    return (prologue,
            [functools.partial(scan_unit, c, g) for c in range(sb_n * nch)
             for g in range(0, SSD_GROUPS, SCAN_UNIT_GROUPS)],
            [functools.partial(gate_unit, g) for g in range(SSD_GROUPS)])


def _ssd_stage_c(yb_r, x_ref, wout_ref, g_ref, b_ref, o_ref, *, sb_n, tt):
    rows = sb_n * tt

    def out_unit(c0):
        mix = jnp.dot(yb_r[...], wout_ref[:, c0:c0 + OUT_BLK], preferred_element_type=F32)
        o_ref[:, :, c0:c0 + OUT_BLK] = mix.reshape(sb_n, tt, OUT_BLK)

    def last():
        mix = o_ref[...].reshape(rows, D_MODEL)
        x = x_ref[...].reshape(rows, D_MODEL)
        o_ref[...] = _layer_norm(DN_ALPHA * x + mix, g_ref[...], b_ref[...]).reshape(sb_n, tt, D_MODEL)

    return [functools.partial(out_unit, c0) for c0 in range(0, D_MODEL, OUT_BLK)], last


def _ssd_trace_order(stage_a, stage_b, stage_c, state_units=()):
    a_first, a_dots, a_convs, a_z = stage_a
    b_pro, b_scans, b_gates = stage_b
    c_units = list(stage_c[0]) + [stage_c[1]]
    lead = 4
    n_blk = len(a_dots)
    order = [a_first] + list(state_units[:1]) + a_dots[0:2] + [b_pro] + a_dots[2:lead]
    nxt_conv, nxt_dot = 0, lead

    def conv_and_dot():
        nonlocal nxt_conv, nxt_dot
        order.append(a_convs[nxt_conv])
        nxt_conv += 1
        if nxt_dot < n_blk:
            order.append(a_dots[nxt_dot])
            nxt_dot += 1

    n_scan = len(b_scans)
    n_beside_scans = 6 * n_scan * SCAN_UNIT_GROUPS // 32
    n_c = len(c_units)
    for u, scan in enumerate(b_scans):
        order.append(scan)
        for _ in range((u + 1) * n_beside_scans // n_scan - u * n_beside_scans // n_scan):
            conv_and_dot()
        for _ in range((u + 1) * n_c // n_scan - u * n_c // n_scan):
            order.append(c_units.pop(0))
    for g, gate in enumerate(b_gates):
        order.append(gate)
        order.extend(a_z[g:g + 1])
        if nxt_conv < n_blk:
            conv_and_dot()
    order.extend(a_z[len(b_gates):])
    while nxt_conv < n_blk:
        conv_and_dot()
    order.extend(c_units)
    n_units = (1 + 2 * n_blk + len(a_z) + 1 + len(b_scans) + len(b_gates) + len(stage_c[0]) + 1
               + len(state_units))
    assert len(order) == n_units and len(set(map(id, order))) == n_units
    return order


def _ssd_kernel(*refs, sb_n, tt, nt, n_tiles, has_state):
    if has_state:
        (x_ref, xres_ref, cprev_ref, h0_ref, *refs) = refs
    else:
        (x_ref, xres_ref, *refs) = refs
    (wz_ref, wdt_ref, cw_ref, cb_ref, dtb_ref, alog_ref, dexp_ref, nw_ref, wout_ref, e2_ref,
     g_ref, b_ref, o_ref, cnew_ref, hout_ref,
     ext_s, dtx_s, acx_s, y_s, ac_s, h_s, yb0_s, yb1_s, *handoffs) = refs
    n_hand = len(handoffs) // 2
    hand = (tuple(handoffs[:n_hand]), tuple(handoffs[n_hand:]))
    yb = (yb0_s, yb1_s)
    i = pl.program_id(0)
    t_a = jnp.minimum(i, n_tiles - 1) % nt
    t_b = jnp.clip(i - 1, 0, n_tiles - 1) % nt
    hist = CONV_PAD - (SSD_CONV - 1)

    @pl.when(i == 0)
    def _zero_first_handoff():
        for ref in hand[1] + (yb[0],):
            ref[...] = jnp.zeros(ref.shape, ref.dtype)

    @pl.when(t_a == 0)
    def _conv_history():
        ext_s[:, 0:CONV_PAD, :] = jnp.zeros((sb_n, CONV_PAD, SSD_CONV_DIM), F32)
        if has_state:
            for sb in range(sb_n):
                ext_s[sb, hist:CONV_PAD, :] = cprev_ref[sb]

    def state_in():
        if has_state:
            for sb in range(sb_n):
                for g in range(SSD_GROUPS):
                    hg = h0_ref[sb, g * HEADS_PER_GROUP:(g + 1) * HEADS_PER_GROUP]
                    h_s[sb, g] = hg.reshape(GROUP_W, SSD_STATE).T
        else:
            h_s[...] = jnp.zeros(h_s.shape, F32)

    inline_state_in = has_state and nt == 1
    state_units = (state_in,) if inline_state_in else ()
    if not inline_state_in:
        pl.when(t_b == 0)(state_in)

    def all_stages(parity):
        stage_a = _ssd_stage_a(x_ref, wz_ref, wdt_ref, cw_ref, cb_ref, ext_s, hand[parity], sb_n=sb_n, tt=tt)
        stage_b = _ssd_stage_b(hand[1 - parity], dtb_ref, alog_ref, dexp_ref, nw_ref, e2_ref, yb[1 - parity],
                               dtx_s, acx_s, y_s, ac_s, h_s, sb_n=sb_n, tt=tt)
        stage_c = _ssd_stage_c(yb[parity], xres_ref, wout_ref, g_ref, b_ref, o_ref, sb_n=sb_n, tt=tt)
        for unit in _ssd_trace_order(stage_a, stage_b, stage_c, state_units):
            unit()

    @pl.when(i % 2 == 0)
    def _even():
        all_stages(0)

    @pl.when(i % 2 == 1)
    def _odd():
        all_stages(1)

    @pl.when((t_a == nt - 1) & (i < n_tiles))
    def _conv_out():
        cnew_ref[...] = ext_s[:, hist:CONV_PAD, :]

    @pl.when((t_b == nt - 1) & (i >= 1) & (i <= n_tiles))
    def _state_out():
        for sb in range(sb_n):
            for g in range(SSD_GROUPS):
                hout_ref[sb, g * HEADS_PER_GROUP:(g + 1) * HEADS_PER_GROUP] = (
                    h_s[sb, g].T.reshape(HEADS_PER_GROUP, SSD_HEAD_DIM, SSD_STATE))


def _ssd_layer(x, conv_prev, h0, w, *, sb_n, tt):
    nseq, seq_len, _ = x.shape
    has_state = h0 is not None
    rows = sb_n * tt
    nt = seq_len // tt
    n_tiles = (nseq // sb_n) * nt

    def tile_a(i):
        return jnp.minimum(i, n_tiles - 1)

    def tile_b(i):
        return jnp.clip(i - 1, 0, n_tiles - 1)

    def tile_c(i):
        return jnp.maximum(i - 2, 0)

    x_spec = pl.BlockSpec((sb_n, tt, D_MODEL), lambda i: (tile_a(i) // nt, tile_a(i) % nt, 0))
    o_spec = pl.BlockSpec((sb_n, tt, D_MODEL), lambda i: (tile_c(i) // nt, tile_c(i) % nt, 0))
    conv_spec = pl.BlockSpec((sb_n, SSD_CONV - 1, SSD_CONV_DIM), lambda i: (tile_a(i) // nt, 0, 0))
    state_spec = pl.BlockSpec((sb_n, SSD_HEADS, SSD_HEAD_DIM, SSD_STATE), lambda i: (tile_b(i) // nt, 0, 0, 0))
    consts = (w["w_zx"], w["w_dt"], w["conv_w"], w["conv_b"], w["dt_bias"], w["a_log"], w["d_exp"],
              w["norm_w"], w["w_out"], w["e2"], w["ln_g"], w["ln_b"])
    in_specs = ([x_spec, o_spec] + ([conv_spec, state_spec] if has_state else [])
                + [_const_spec(c.shape) for c in consts])
    args = (x, x) + ((conv_prev, h0) if has_state else ()) + consts
    handoff = [
        pltpu.VMEM((rows, SSD_D_INNER), F32),
        pltpu.VMEM((rows, SSD_D_INNER), F32),
        pltpu.VMEM((rows, 2 * SSD_BC), BF16),
        pltpu.VMEM((rows, LANES), F32),
    ]
    scratch = [
        pltpu.VMEM((sb_n, tt + CONV_PAD, SSD_CONV_DIM), F32),
        pltpu.VMEM((rows, SSD_D_INNER), F32),
        pltpu.VMEM((rows, SSD_D_INNER), F32),
        pltpu.VMEM((rows, SSD_D_INNER), F32),
        pltpu.VMEM((rows, LANES), F32),
        pltpu.VMEM((sb_n, SSD_GROUPS, SSD_STATE, GROUP_W), F32),
        pltpu.VMEM((rows, SSD_D_INNER), BF16),
        pltpu.VMEM((rows, SSD_D_INNER), BF16),
    ] + handoff + handoff
    out_shape = (
        jax.ShapeDtypeStruct(x.shape, F32),
        jax.ShapeDtypeStruct((nseq, SSD_CONV - 1, SSD_CONV_DIM), F32),
        jax.ShapeDtypeStruct((nseq, SSD_HEADS, SSD_HEAD_DIM, SSD_STATE), F32),
    )
    return pl.pallas_call(
        functools.partial(_ssd_kernel, sb_n=sb_n, tt=tt, nt=nt, n_tiles=n_tiles, has_state=has_state),
        grid=(n_tiles + 2,),
        in_specs=in_specs,
        out_specs=(o_spec, conv_spec, state_spec),
        out_shape=out_shape,
        scratch_shapes=scratch,
        compiler_params=pltpu.CompilerParams(dimension_semantics=("arbitrary",), vmem_limit_bytes=VMEM_LIMIT),
        name="ssd_state" if has_state else "ssd_fresh",
    )(*args)


def _ffn_kernel(x_ref, w1_ref, w2_ref, g_ref, b_ref, o_ref, sum_s, *, n_tiles):
    i = pl.program_id(0)

    def mlp():
        x = x_ref[...]
        h = jnp.dot(x.astype(BF16), w1_ref[...], preferred_element_type=F32)
        h = jnp.maximum(h, 0.0)
        y = jnp.dot((h * h).astype(BF16), w2_ref[...], preferred_element_type=F32)
        sum_s[...] = DN_ALPHA * x + y

    def norm():
        o_ref[...] = _layer_norm(sum_s[...], g_ref[...], b_ref[...])

    @pl.when(i == 0)
    def _first():
        mlp()

    @pl.when((i > 0) & (i < n_tiles))
    def _steady():
        norm()
        mlp()

    @pl.when(i == n_tiles)
    def _last():
        norm()


def _ffn_layer(x2d, w1, w2, g, b, *, layer, tm):
    n_tiles = x2d.shape[0] // tm
    x_spec = pl.BlockSpec((tm, D_MODEL), lambda i: (jnp.minimum(i, n_tiles - 1), 0))
    o_spec = pl.BlockSpec((tm, D_MODEL), lambda i: (jnp.maximum(i - 1, 0), 0))

    def layer_spec(shape):
        return pl.BlockSpec((None,) + shape[1:], lambda i: (layer, 0, 0), pipeline_mode=pl.Buffered(1))

    return pl.pallas_call(
        functools.partial(_ffn_kernel, n_tiles=n_tiles),
        grid=(n_tiles + 1,),
        in_specs=[x_spec, layer_spec(w1.shape), layer_spec(w2.shape), _const_spec(g.shape), _const_spec(b.shape)],
        out_specs=o_spec,
        out_shape=jax.ShapeDtypeStruct(x2d.shape, F32),
        scratch_shapes=[pltpu.VMEM((tm, D_MODEL), F32)],
        compiler_params=pltpu.CompilerParams(dimension_semantics=("arbitrary",), vmem_limit_bytes=VMEM_LIMIT),
        name="ffn",
    )(x2d, w1, w2, g, b)


def _cmlp_kernel(x_ref, win_ref, bin_ref, lng_ref, lnb_ref, ws_ref, bs_ref, wout_ref, g_ref, b_ref,
                 o_ref, *rest, blk, emit_v):
    if emit_v:
        v_ref, u_s, vpre_s, v_s, ub_s = rest
    else:
        u_s, vpre_s, v_s, ub_s = rest
    tm = x_ref.shape[0]
    xb = x_ref[...].astype(BF16)

    def dot_unit(dst, c_src, c_dst):
        dst[:, c_dst:c_dst + COL_BLK_CM] = jnp.dot(xb, win_ref[:, c_src:c_src + COL_BLK_CM],
                                                   preferred_element_type=F32)

    def gelu_unit(dst, c_src, c_dst):
        h = dst[:, c_dst:c_dst + COL_BLK_CM] + bin_ref[:, c_src:c_src + COL_BLK_CM]
        half = 0.5 * h
        inner = h * (GELU_K + (GELU_K * GELU_C) * (h * h))
        dst[:, c_dst:c_dst + COL_BLK_CM] = half + half * jnp.tanh(inner)

    def norm_v():
        v = _layer_norm(vpre_s[...], lng_ref[...], lnb_ref[...])
        if emit_v:
            v_ref[...] = v
        v_s[...] = v.astype(BF16)

    def spatial_unit(g):
        ri = lax.broadcasted_iota(jnp.int32, (blk, blk), 0)
        ci = lax.broadcasted_iota(jnp.int32, (blk, blk), 1)
        gs = slice(g * CM_GROUP_W, (g + 1) * CM_GROUP_W)
        ws_g = jnp.where(ci <= ri, ws_ref[g, 0:blk, 0:blk], 0.0).astype(BF16)
        for i in range(tm // blk):
            rs = slice(i * blk, (i + 1) * blk)
            s = jnp.dot(ws_g, v_s[rs, gs], preferred_element_type=F32) + bs_ref[0:blk, gs]
            ub_s[rs, gs] = (u_s[rs, gs] * s).astype(BF16)

    def out_unit(c0):
        o_ref[:, c0:c0 + OUT_BLK] = jnp.dot(ub_s[...], wout_ref[:, c0:c0 + OUT_BLK],
                                            preferred_element_type=F32)

    def last():
        o_ref[...] = _layer_norm(DN_ALPHA * x_ref[...] + o_ref[...], g_ref[...], b_ref[...])

    n_blk = CM_WIDTH // COL_BLK_CM
    dots = ([functools.partial(dot_unit, vpre_s, CM_WIDTH + k * COL_BLK_CM, k * COL_BLK_CM) for k in range(n_blk)]
            + [functools.partial(dot_unit, u_s, k * COL_BLK_CM, k * COL_BLK_CM) for k in range(n_blk)])
    gelus = ([functools.partial(gelu_unit, vpre_s, CM_WIDTH + k * COL_BLK_CM, k * COL_BLK_CM) for k in range(n_blk)]
             + [functools.partial(gelu_unit, u_s, k * COL_BLK_CM, k * COL_BLK_CM) for k in range(n_blk)])
    spatial = [functools.partial(spatial_unit, g) for g in range(CM_GROUPS)]
    groups_per_blk = COL_BLK_CM // CM_GROUP_W
    order = [dots[0]]
    for k in range(2 * n_blk):
        order.extend(dots[k + 1:k + 2])
        order.append(gelus[k])
        if k == n_blk - 1:
            order.append(norm_v)
        if k >= n_blk:
            u_blk = k - n_blk
            order.extend(spatial[u_blk * groups_per_blk:(u_blk + 1) * groups_per_blk])
    order.extend([functools.partial(out_unit, c0) for c0 in range(0, D_MODEL, OUT_BLK)])
    order.append(last)
    for unit in order:
        unit()


def _cmlp_layer(x2d, w, *, tm, blk, emit_v):
    n = x2d.shape[0]
    row_spec = pl.BlockSpec((tm, D_MODEL), lambda i: (i, 0))
    consts = (w["w_in"], w["b_in"], w["ln_g"], w["ln_b"], w["w_s"], w["b_s"], w["w_out"], w["ln1_g"], w["ln1_b"])
    out_shape = [jax.ShapeDtypeStruct(x2d.shape, F32)]
    out_specs = [row_spec]
    if emit_v:
        out_shape.append(jax.ShapeDtypeStruct((n, CM_WIDTH), F32))
        out_specs.append(pl.BlockSpec((tm, CM_WIDTH), lambda i: (i, 0)))
    return pl.pallas_call(
        functools.partial(_cmlp_kernel, blk=blk, emit_v=emit_v),
        grid=(n // tm,),
        in_specs=[row_spec] + [_const_spec(c.shape) for c in consts],
        out_specs=out_specs,
        out_shape=out_shape,
        scratch_shapes=[pltpu.VMEM((tm, CM_WIDTH), F32),
                        pltpu.VMEM((tm, CM_WIDTH), F32),
                        pltpu.VMEM((tm, CM_WIDTH), BF16),
                        pltpu.VMEM((tm, CM_WIDTH), BF16)],
        compiler_params=pltpu.CompilerParams(dimension_semantics=("arbitrary",), vmem_limit_bytes=VMEM_LIMIT),
        name="cmlp_v" if emit_v else "cmlp",
    )(x2d, *consts)


def kernel(x_prompt, x_sample, state_ssm, state_conv, ssd_w_in, ssd_conv_w, ssd_conv_b, ssd_dt_bias, ssd_a_log, ssd_d, ssd_norm_w, ssd_w_out, cm_w_in, cm_b_in, cm_ln_g, cm_ln_b, cm_w_s, cm_b_s, cm_w_out, ffn_w1, ffn_w2, ln1_g, ln1_b, ln2_g, ln2_b):
    row = lambda a: a.reshape(1, -1)
    perm = jnp.asarray(HEAD_PERM)

    def head_lanes(a):
        return jnp.pad(a[perm], (0, LANES - SSD_HEADS)).reshape(1, LANES)

    w_in = ssd_w_in[0]
    zx_end = SSD_D_INNER + SSD_CONV_DIM
    ssd_w = {
        "w_zx": w_in.astype(BF16),
        "w_dt": jnp.pad(w_in[:, zx_end:][:, perm], ((0, 0), (0, LANES - SSD_HEADS))).astype(BF16),
        "conv_w": ssd_conv_w[0],
        "conv_b": row(ssd_conv_b[0]),
        "dt_bias": head_lanes(ssd_dt_bias[0]),
        "a_log": head_lanes(ssd_a_log[0]),
        "d_exp": row(jnp.repeat(ssd_d[0], SSD_HEAD_DIM)),
        "norm_w": row(ssd_norm_w[0]),
        "w_out": ssd_w_out[0].astype(BF16),
        "e2": jnp.asarray(_expand_matrix(), BF16),
        "ln_g": row(ln1_g[0]),
        "ln_b": row(ln1_b[0]),
    }
    cm_w = {
        "w_in": cm_w_in[0].astype(BF16),
        "b_in": row(cm_b_in[0]),
        "ln_g": row(cm_ln_g[0]),
        "ln_b": row(cm_ln_b[0]),
        "w_s": cm_w_s[0],
        "b_s": jnp.repeat(cm_b_s[0].T, CM_GROUP_W, axis=1),
        "w_out": cm_w_out[0].astype(BF16),
        "ln1_g": row(ln1_g[1]),
        "ln1_b": row(ln1_b[1]),
    }
    ffn_w1b, ffn_w2b = ffn_w1.astype(BF16), ffn_w2.astype(BF16)
    ffn = [functools.partial(_ffn_layer, w1=ffn_w1b, w2=ffn_w2b, g=row(ln2_g[i]), b=row(ln2_b[i]), layer=i, tm=512)
           for i in range(DEPTH)]

    def trunk(x, conv_prev, h0, *, sb_n, tt, blk, emit_v):
        shape = x.shape
        x1, conv_new, h_new = _ssd_layer(x, conv_prev, h0, ssd_w, sb_n=sb_n, tt=tt)
        x2 = ffn[0](x1.reshape(-1, D_MODEL))
        res = _cmlp_layer(x2, cm_w, tm=512, blk=blk, emit_v=emit_v)
        x4 = ffn[1](res[0])
        v_rows = res[1].reshape(shape[0], shape[1], CM_WIDTH) if emit_v else None
        return x4.reshape(shape), conv_new, h_new, v_rows

    y_p, conv_p, h_p, _ = trunk(x_prompt, None, None, sb_n=1, tt=256, blk=CM_BLOCK, emit_v=False)
    y_s, conv_s, h_s, v_s = trunk(x_sample, state_conv[0], state_ssm[0], sb_n=2, tt=CHUNK,
                                  blk=x_sample.shape[1], emit_v=True)
    return (y_p, y_s, h_p[None], conv_p[None], h_s[None], conv_s[None], v_s[None])
```

```python
import functools

import numpy as np
import jax
import jax.numpy as jnp
from jax import lax
from jax.experimental import pallas as pl
from jax.experimental.pallas import tpu as pltpu

F32 = jnp.float32
BF16 = jnp.bfloat16

D_MODEL = 1024
DEPTH = 2
CHUNK = 64
SSD_D_INNER = 2 * D_MODEL
SSD_HEAD_DIM = 64
SSD_HEADS = SSD_D_INNER // SSD_HEAD_DIM
SSD_GROUPS = 8
SSD_STATE = 128
SSD_CONV = 4
SSD_BC = SSD_GROUPS * SSD_STATE
SSD_CONV_DIM = SSD_D_INNER + 2 * SSD_BC
HEADS_PER_GROUP = SSD_HEADS // SSD_GROUPS
GROUP_W = HEADS_PER_GROUP * SSD_HEAD_DIM
NORM_GROUP_W = SSD_D_INNER // SSD_GROUPS
CM_BLOCK = 128
CM_WIDTH = 2 * D_MODEL
CM_GROUPS = 8
CM_GROUP_W = CM_WIDTH // CM_GROUPS
FFN_HIDDEN = 4 * D_MODEL
DN_ALPHA = (2 * DEPTH) ** 0.25
LN_EPS = 1e-5
LOG2_E = 1.4426950408889634
GELU_K = 0.7978845608028654
GELU_C = 0.044715

LANES = 128
CONV_PAD = 8
COL_BLK = 256
OUT_BLK = 256
COL_BLK_CM = 256
SCAN_UNIT_GROUPS = 4
VMEM_LIMIT = 56 * 1024 * 1024

HEAD_PERM = np.concatenate([np.arange(0, SSD_HEADS, 2), np.arange(1, SSD_HEADS, 2)])


def _expand_matrix():
    e = np.zeros((2 * LANES, SSD_D_INNER), np.float32)
    for k, h in enumerate(HEAD_PERM):
        e[k, h * SSD_HEAD_DIM:(h + 1) * SSD_HEAD_DIM] = 1.0
        e[LANES + k, h * SSD_HEAD_DIM:(h + 1) * SSD_HEAD_DIM] = 1.0
    return e


def _layer_norm(x, g, b):
    mu = jnp.mean(x, axis=-1, keepdims=True)
    xc = x - mu
    var = jnp.mean(xc * xc, axis=-1, keepdims=True)
    return xc * lax.rsqrt(var + LN_EPS) * g + b


def _silu(x):
    half = 0.5 * x
    return half + half * jnp.tanh(half)


def _split2(v):
    hi = v.astype(BF16)
    lo = (v - hi.astype(F32)).astype(BF16)
    return jnp.concatenate([hi, lo], axis=1)


def _const_spec(shape):
    nd = len(shape)
    return pl.BlockSpec(shape, lambda *_: (0,) * nd, pipeline_mode=pl.Buffered(1))


def _ssd_stage_a(x_ref, wz_ref, wdt_ref, cw_ref, cb_ref, ext_s, handoff, *, sb_n, tt):
    assert SSD_CONV == 4
    z_w, xs_w, bc_w, dtr_w = handoff
    rows = sb_n * tt
    hist = CONV_PAD - (SSD_CONV - 1)
    live = {}

    def first():
        live["xb"] = x_ref[...].reshape(rows, D_MODEL).astype(BF16)
        dtr_w[...] = jnp.dot(live["xb"], wdt_ref[...], preferred_element_type=F32)

    def dot_unit(c0):
        cs = slice(c0, c0 + COL_BLK)
        raw = jnp.dot(live["xb"], wz_ref[:, SSD_D_INNER + c0:SSD_D_INNER + c0 + COL_BLK],
                      preferred_element_type=F32)
        for sb in range(sb_n):
            ext_s[sb, CONV_PAD:CONV_PAD + tt, cs] = raw[sb * tt:(sb + 1) * tt]

    def conv_unit(c0):
        cs = slice(c0, c0 + COL_BLK)
        for sb in range(sb_n):
            ext = ext_s[sb, :, cs]
            back1 = pltpu.roll(ext, 1, axis=0)
            older = ext * cw_ref[1:2, cs] + back1 * cw_ref[0:1, cs]
            acc = (cb_ref[:, cs] + ext * cw_ref[3:4, cs] + back1 * cw_ref[2:3, cs]
                   + pltpu.roll(older, 2, axis=0))[CONV_PAD:]
            act = _silu(acc)
            rs = slice(sb * tt, (sb + 1) * tt)
            if c0 < SSD_D_INNER:
                xs_w[rs, cs] = act
            else:
                bc_w[rs, c0 - SSD_D_INNER:c0 - SSD_D_INNER + COL_BLK] = act.astype(BF16)
            ext_s[sb, hist:CONV_PAD, cs] = ext[tt + hist:tt + CONV_PAD]

    def z_unit(c0):
        z_w[:, c0:c0 + COL_BLK] = jnp.dot(live["xb"], wz_ref[:, c0:c0 + COL_BLK], preferred_element_type=F32)

    blocks = range(0, SSD_CONV_DIM, COL_BLK)
    return (first,
            [functools.partial(dot_unit, c0) for c0 in blocks],
            [functools.partial(conv_unit, c0) for c0 in blocks],
            [functools.partial(z_unit, c0) for c0 in range(0, SSD_D_INNER, COL_BLK)])


def _ssd_stage_b(handoff, dtb_ref, alog_ref, dexp_ref, nw_ref, e2_ref, yb_w,
                 dtx_s, acx_s, y_s, ac_s, h_s, *, sb_n, tt):
    z_r, xs_r, bc_r, dtr_r = handoff
    rows = sb_n * tt
    nch = tt // CHUNK
    n_pairs = SSD_HEADS // 2
    live = {}

    def prologue():
        v = dtr_r[...] + dtb_ref[...]
        dt = jnp.maximum(v, 0.0) + jnp.log1p(jnp.exp(-jnp.abs(v)))
        a_neg = -jnp.exp(alog_ref[...])
        d_a = dt * a_neg
        ri = lax.broadcasted_iota(jnp.int32, (rows, rows), 0)
        ci = lax.broadcasted_iota(jnp.int32, (rows, rows), 1)
        same_chunk = (ri & -CHUNK) == (ci & -CHUNK)
        tril = jnp.where((ci <= ri) & same_chunk, 1.0, 0.0).astype(BF16)
        cs = jnp.dot(tril, _split2(d_a), preferred_element_type=F32)
        acum = (cs[:, 0:LANES] + cs[:, LANES:]) * LOG2_E
        ac_s[...] = acum
        acx_s[...] = jnp.dot(_split2(acum), e2_ref[...], preferred_element_type=F32)
        dtx_s[...] = xs_r[...] * jnp.dot(_split2(dt), e2_ref[...], preferred_element_type=F32)
        row_i = lax.broadcasted_iota(jnp.int32, (CHUNK, LANES), 0)
        lane_i = lax.broadcasted_iota(jnp.int32, (CHUNK, LANES), 1)
        live["causal2"] = (lane_i & (CHUNK - 1)) <= row_i
        live["first_half"] = lane_i < CHUNK

    def scan_unit(chunk, g0):
        sb = chunk // nch
        r0 = chunk * CHUNK
        rs = slice(r0, r0 + CHUNK)
        if g0 == 0:
            ac_t = ac_s[rs, :].T
            live["pair_rows"] = jnp.concatenate([ac_t[0:n_pairs], ac_t[n_pairs:2 * n_pairs]], axis=1)
        pair_rows, causal2, first_half = live["pair_rows"], live["causal2"], live["first_half"]
        groups = range(g0, g0 + SCAN_UNIT_GROUPS)
        b, c, h, acx, cb2, y_off = {}, {}, {}, {}, {}, {}
        for g in groups:
            gs = slice(g * GROUP_W, (g + 1) * GROUP_W)
            b[g] = bc_r[rs, g * SSD_STATE:(g + 1) * SSD_STATE]
            c[g] = bc_r[rs, SSD_BC + g * SSD_STATE:SSD_BC + (g + 1) * SSD_STATE]
            bb = jnp.concatenate([b[g], b[g]], axis=0)
            cb2[g] = lax.dot_general(c[g], bb, (((1,), (1,)), ((), ())), preferred_element_type=F32)
            h[g] = h_s[sb, g]
            acx[g] = acx_s[rs, gs]
            y_off[g] = jnp.dot(c[g], h[g].astype(BF16), preferred_element_type=F32)
        y_diag = {}
        for g in groups:
            for jj in range(2):
                j = 2 * g + jj
                ls = slice(j * LANES, (j + 1) * LANES)
                seg = acx_s[rs, ls] - pair_rows[j:j + 1, :]
                decay = jnp.where(causal2, jnp.exp2(seg), 0.0)
                m2 = (cb2[g] * decay).astype(BF16)
                xp = dtx_s[rs, ls]
                xbd = jnp.concatenate([jnp.where(first_half, xp, 0.0), jnp.where(first_half, 0.0, xp)], axis=0)
                y_diag[j] = jnp.dot(m2, xbd.astype(BF16), preferred_element_type=F32)
        upd, a_last = {}, {}
        for g in groups:
            gs = slice(g * GROUP_W, (g + 1) * GROUP_W)
            a_last[g] = acx_s[r0 + CHUNK - 1:r0 + CHUNK, gs]
            xw = (dtx_s[rs, gs] * jnp.exp2(a_last[g] - acx[g])).astype(BF16)
            upd[g] = lax.dot_general(b[g], xw, (((0,), (0,)), ((), ())), preferred_element_type=F32)
        for g in groups:
            y_g = jnp.concatenate([y_diag[2 * g], y_diag[2 * g + 1]], axis=1) + y_off[g] * jnp.exp2(acx[g])
            y_s[rs, g * GROUP_W:(g + 1) * GROUP_W] = y_g
            h_s[sb, g] = h[g] * jnp.exp2(a_last[g]) + upd[g]

    def gate_unit(g):
        gs = slice(g * NORM_GROUP_W, (g + 1) * NORM_GROUP_W)
        y = y_s[:, gs] + dexp_ref[:, gs] * xs_r[:, gs]
        zg = z_r[:, gs]
        gg = y * _silu(zg)
        ms = jnp.mean(gg * gg, axis=-1, keepdims=True)
        yb_w[:, gs] = (gg * lax.rsqrt(ms + LN_EPS) * nw_ref[:, gs]).astype(BF16)

    return (prologue,
            [functools.partial(scan_unit, c, g) for c in range(sb_n * nch)
             for g in range(0, SSD_GROUPS, SCAN_UNIT_GROUPS)],
            [functools.partial(gate_unit, g) for g in range(SSD_GROUPS)])


def _ssd_stage_c(yb_r, x_ref, wout_ref, g_ref, b_ref, o_ref, *, sb_n, tt):
    rows = sb_n * tt

    def out_unit(c0):
        mix = jnp.dot(yb_r[...], wout_ref[:, c0:c0 + OUT_BLK], preferred_element_type=F32)
        o_ref[:, :, c0:c0 + OUT_BLK] = mix.reshape(sb_n, tt, OUT_BLK)

    def last():
        mix = o_ref[...].reshape(rows, D_MODEL)
        x = x_ref[...].reshape(rows, D_MODEL)
        o_ref[...] = _layer_norm(DN_ALPHA * x + mix, g_ref[...], b_ref[...]).reshape(sb_n, tt, D_MODEL)

    return [functools.partial(out_unit, c0) for c0 in range(0, D_MODEL, OUT_BLK)], last


def _ssd_trace_order(stage_a, stage_b, stage_c, state_units=()):
    a_first, a_dots, a_convs, a_z = stage_a
    b_pro, b_scans, b_gates = stage_b
    c_units = list(stage_c[0]) + [stage_c[1]]
    lead = 4
    n_blk = len(a_dots)
    order = [a_first] + list(state_units[:1]) + a_dots[0:2] + [b_pro] + a_dots[2:lead]
    nxt_conv, nxt_dot = 0, lead

    def conv_and_dot():
        nonlocal nxt_conv, nxt_dot
        order.append(a_convs[nxt_conv])
        nxt_conv += 1
        if nxt_dot < n_blk:
            order.append(a_dots[nxt_dot])
            nxt_dot += 1

    n_scan = len(b_scans)
    n_beside_scans = 6 * n_scan * SCAN_UNIT_GROUPS // 32
    n_c = len(c_units)
    for u, scan in enumerate(b_scans):
        order.append(scan)
        for _ in range((u + 1) * n_beside_scans // n_scan - u * n_beside_scans // n_scan):
            conv_and_dot()
        for _ in range((u + 1) * n_c // n_scan - u * n_c // n_scan):
            order.append(c_units.pop(0))
    for g, gate in enumerate(b_gates):
        order.append(gate)
        order.extend(a_z[g:g + 1])
        if nxt_conv < n_blk:
            conv_and_dot()
    order.extend(a_z[len(b_gates):])
    while nxt_conv < n_blk:
        conv_and_dot()
    order.extend(c_units)
    n_units = (1 + 2 * n_blk + len(a_z) + 1 + len(b_scans) + len(b_gates) + len(stage_c[0]) + 1
               + len(state_units))
    assert len(order) == n_units and len(set(map(id, order))) == n_units
    return order


def _ssd_kernel(*refs, sb_n, tt, nt, n_tiles, has_state):
    if has_state:
        (x_ref, xres_ref, cprev_ref, h0_ref, *refs) = refs
    else:
        (x_ref, xres_ref, *refs) = refs
    (wz_ref, wdt_ref, cw_ref, cb_ref, dtb_ref, alog_ref, dexp_ref, nw_ref, wout_ref, e2_ref,
     g_ref, b_ref, o_ref, cnew_ref, hout_ref,
     ext_s, dtx_s, acx_s, y_s, ac_s, h_s, yb0_s, yb1_s, *handoffs) = refs
    n_hand = len(handoffs) // 2
    hand = (tuple(handoffs[:n_hand]), tuple(handoffs[n_hand:]))
    yb = (yb0_s, yb1_s)
    i = pl.program_id(0)
    t_a = jnp.minimum(i, n_tiles - 1) % nt
    t_b = jnp.clip(i - 1, 0, n_tiles - 1) % nt
    hist = CONV_PAD - (SSD_CONV - 1)

    @pl.when(i == 0)
    def _zero_first_handoff():
        for ref in hand[1] + (yb[0],):
            ref[...] = jnp.zeros(ref.shape, ref.dtype)

    @pl.when(t_a == 0)
    def _conv_history():
        ext_s[:, 0:CONV_PAD, :] = jnp.zeros((sb_n, CONV_PAD, SSD_CONV_DIM), F32)
        if has_state:
            for sb in range(sb_n):
                ext_s[sb, hist:CONV_PAD, :] = cprev_ref[sb]

    def state_in():
        if has_state:
            for sb in range(sb_n):
                for g in range(SSD_GROUPS):
                    hg = h0_ref[sb, g * HEADS_PER_GROUP:(g + 1) * HEADS_PER_GROUP]
                    h_s[sb, g] = hg.reshape(GROUP_W, SSD_STATE).T
        else:
            h_s[...] = jnp.zeros(h_s.shape, F32)

    inline_state_in = has_state and nt == 1
    state_units = (state_in,) if inline_state_in else ()
    if not inline_state_in:
        pl.when(t_b == 0)(state_in)

    def all_stages(parity):
        stage_a = _ssd_stage_a(x_ref, wz_ref, wdt_ref, cw_ref, cb_ref, ext_s, hand[parity], sb_n=sb_n, tt=tt)
        stage_b = _ssd_stage_b(hand[1 - parity], dtb_ref, alog_ref, dexp_ref, nw_ref, e2_ref, yb[1 - parity],
                               dtx_s, acx_s, y_s, ac_s, h_s, sb_n=sb_n, tt=tt)
        stage_c = _ssd_stage_c(yb[parity], xres_ref, wout_ref, g_ref, b_ref, o_ref, sb_n=sb_n, tt=tt)
        for unit in _ssd_trace_order(stage_a, stage_b, stage_c, state_units):
            unit()

    @pl.when(i % 2 == 0)
    def _even():
        all_stages(0)

    @pl.when(i % 2 == 1)
    def _odd():
        all_stages(1)

    @pl.when((t_a == nt - 1) & (i < n_tiles))
    def _conv_out():
        cnew_ref[...] = ext_s[:, hist:CONV_PAD, :]

    @pl.when((t_b == nt - 1) & (i >= 1) & (i <= n_tiles))
    def _state_out():
        for sb in range(sb_n):
            for g in range(SSD_GROUPS):
                hout_ref[sb, g * HEADS_PER_GROUP:(g + 1) * HEADS_PER_GROUP] = (
                    h_s[sb, g].T.reshape(HEADS_PER_GROUP, SSD_HEAD_DIM, SSD_STATE))


def _ssd_layer(x, conv_prev, h0, w, *, sb_n, tt):
    nseq, seq_len, _ = x.shape
    has_state = h0 is not None
    rows = sb_n * tt
    nt = seq_len // tt
    n_tiles = (nseq // sb_n) * nt

    def tile_a(i):
        return jnp.minimum(i, n_tiles - 1)

    def tile_b(i):
        return jnp.clip(i - 1, 0, n_tiles - 1)

    def tile_c(i):
        return jnp.maximum(i - 2, 0)

    x_spec = pl.BlockSpec((sb_n, tt, D_MODEL), lambda i: (tile_a(i) // nt, tile_a(i) % nt, 0))
    o_spec = pl.BlockSpec((sb_n, tt, D_MODEL), lambda i: (tile_c(i) // nt, tile_c(i) % nt, 0))
    conv_spec = pl.BlockSpec((sb_n, SSD_CONV - 1, SSD_CONV_DIM), lambda i: (tile_a(i) // nt, 0, 0))
    state_spec = pl.BlockSpec((sb_n, SSD_HEADS, SSD_HEAD_DIM, SSD_STATE), lambda i: (tile_b(i) // nt, 0, 0, 0))
    consts = (w["w_zx"], w["w_dt"], w["conv_w"], w["conv_b"], w["dt_bias"], w["a_log"], w["d_exp"],
              w["norm_w"], w["w_out"], w["e2"], w["ln_g"], w["ln_b"])
    in_specs = ([x_spec, o_spec] + ([conv_spec, state_spec] if has_state else [])
                + [_const_spec(c.shape) for c in consts])
    args = (x, x) + ((conv_prev, h0) if has_state else ()) + consts
    handoff = [
        pltpu.VMEM((rows, SSD_D_INNER), F32),
        pltpu.VMEM((rows, SSD_D_INNER), F32),
        pltpu.VMEM((rows, 2 * SSD_BC), BF16),
        pltpu.VMEM((rows, LANES), F32),
    ]
    scratch = [
        pltpu.VMEM((sb_n, tt + CONV_PAD, SSD_CONV_DIM), F32),
        pltpu.VMEM((rows, SSD_D_INNER), F32),
        pltpu.VMEM((rows, SSD_D_INNER), F32),
        pltpu.VMEM((rows, SSD_D_INNER), F32),
        pltpu.VMEM((rows, LANES), F32),
        pltpu.VMEM((sb_n, SSD_GROUPS, SSD_STATE, GROUP_W), F32),
        pltpu.VMEM((rows, SSD_D_INNER), BF16),
        pltpu.VMEM((rows, SSD_D_INNER), BF16),
    ] + handoff + handoff
    out_shape = (
        jax.ShapeDtypeStruct(x.shape, F32),
        jax.ShapeDtypeStruct((nseq, SSD_CONV - 1, SSD_CONV_DIM), F32),
        jax.ShapeDtypeStruct((nseq, SSD_HEADS, SSD_HEAD_DIM, SSD_STATE), F32),
    )
    return pl.pallas_call(
        functools.partial(_ssd_kernel, sb_n=sb_n, tt=tt, nt=nt, n_tiles=n_tiles, has_state=has_state),
        grid=(n_tiles + 2,),
        in_specs=in_specs,
        out_specs=(o_spec, conv_spec, state_spec),
        out_shape=out_shape,
        scratch_shapes=scratch,
        compiler_params=pltpu.CompilerParams(dimension_semantics=("arbitrary",), vmem_limit_bytes=VMEM_LIMIT),
        name="ssd_state" if has_state else "ssd_fresh",
    )(*args)


def _ffn_kernel(x_ref, w1_ref, w2_ref, g_ref, b_ref, o_ref, sum_s, *, n_tiles):
    i = pl.program_id(0)

    def mlp():
        x = x_ref[...]
        h = jnp.dot(x.astype(BF16), w1_ref[...], preferred_element_type=F32)
        h = jnp.maximum(h, 0.0)
        y = jnp.dot((h * h).astype(BF16), w2_ref[...], preferred_element_type=F32)
        sum_s[...] = DN_ALPHA * x + y

    def norm():
        o_ref[...] = _layer_norm(sum_s[...], g_ref[...], b_ref[...])

    @pl.when(i == 0)
    def _first():
        mlp()

    @pl.when((i > 0) & (i < n_tiles))
    def _steady():
        norm()
        mlp()

    @pl.when(i == n_tiles)
    def _last():
        norm()


def _ffn_layer(x2d, w1, w2, g, b, *, layer, tm):
    n_tiles = x2d.shape[0] // tm
    x_spec = pl.BlockSpec((tm, D_MODEL), lambda i: (jnp.minimum(i, n_tiles - 1), 0))
    o_spec = pl.BlockSpec((tm, D_MODEL), lambda i: (jnp.maximum(i - 1, 0), 0))

    def layer_spec(shape):
        return pl.BlockSpec((None,) + shape[1:], lambda i: (layer, 0, 0), pipeline_mode=pl.Buffered(1))

    return pl.pallas_call(
        functools.partial(_ffn_kernel, n_tiles=n_tiles),
        grid=(n_tiles + 1,),
        in_specs=[x_spec, layer_spec(w1.shape), layer_spec(w2.shape), _const_spec(g.shape), _const_spec(b.shape)],
        out_specs=o_spec,
        out_shape=jax.ShapeDtypeStruct(x2d.shape, F32),
        scratch_shapes=[pltpu.VMEM((tm, D_MODEL), F32)],
        compiler_params=pltpu.CompilerParams(dimension_semantics=("arbitrary",), vmem_limit_bytes=VMEM_LIMIT),
        name="ffn",
    )(x2d, w1, w2, g, b)


def _cmlp_kernel(x_ref, win_ref, bin_ref, lng_ref, lnb_ref, ws_ref, bs_ref, wout_ref, g_ref, b_ref,
                 o_ref, *rest, blk, emit_v):
    if emit_v:
        v_ref, u_s, vpre_s, v_s, ub_s = rest
    else:
        u_s, vpre_s, v_s, ub_s = rest
    tm = x_ref.shape[0]
    xb = x_ref[...].astype(BF16)

    def dot_unit(dst, c_src, c_dst):
        dst[:, c_dst:c_dst + COL_BLK_CM] = jnp.dot(xb, win_ref[:, c_src:c_src + COL_BLK_CM],
                                                   preferred_element_type=F32)

    def gelu_unit(dst, c_src, c_dst):
        h = dst[:, c_dst:c_dst + COL_BLK_CM] + bin_ref[:, c_src:c_src + COL_BLK_CM]
        half = 0.5 * h
        inner = h * (GELU_K + (GELU_K * GELU_C) * (h * h))
        dst[:, c_dst:c_dst + COL_BLK_CM] = half + half * jnp.tanh(inner)

    def norm_v():
        v = _layer_norm(vpre_s[...], lng_ref[...], lnb_ref[...])
        if emit_v:
            v_ref[...] = v
        v_s[...] = v.astype(BF16)

    def spatial_unit(g):
        ri = lax.broadcasted_iota(jnp.int32, (blk, blk), 0)
        ci = lax.broadcasted_iota(jnp.int32, (blk, blk), 1)
        gs = slice(g * CM_GROUP_W, (g + 1) * CM_GROUP_W)
        ws_g = jnp.where(ci <= ri, ws_ref[g, 0:blk, 0:blk], 0.0).astype(BF16)
        for i in range(tm // blk):
            rs = slice(i * blk, (i + 1) * blk)
            s = jnp.dot(ws_g, v_s[rs, gs], preferred_element_type=F32) + bs_ref[0:blk, gs]
            ub_s[rs, gs] = (u_s[rs, gs] * s).astype(BF16)

    def out_unit(c0):
        o_ref[:, c0:c0 + OUT_BLK] = jnp.dot(ub_s[...], wout_ref[:, c0:c0 + OUT_BLK],
                                            preferred_element_type=F32)

    def last():
        o_ref[...] = _layer_norm(DN_ALPHA * x_ref[...] + o_ref[...], g_ref[...], b_ref[...])

    n_blk = CM_WIDTH // COL_BLK_CM
    dots = ([functools.partial(dot_unit, vpre_s, CM_WIDTH + k * COL_BLK_CM, k * COL_BLK_CM) for k in range(n_blk)]
            + [functools.partial(dot_unit, u_s, k * COL_BLK_CM, k * COL_BLK_CM) for k in range(n_blk)])
    gelus = ([functools.partial(gelu_unit, vpre_s, CM_WIDTH + k * COL_BLK_CM, k * COL_BLK_CM) for k in range(n_blk)]
             + [functools.partial(gelu_unit, u_s, k * COL_BLK_CM, k * COL_BLK_CM) for k in range(n_blk)])
    spatial = [functools.partial(spatial_unit, g) for g in range(CM_GROUPS)]
    groups_per_blk = COL_BLK_CM // CM_GROUP_W
    order = [dots[0]]
    for k in range(2 * n_blk):
        order.extend(dots[k + 1:k + 2])
        order.append(gelus[k])
        if k == n_blk - 1:
            order.append(norm_v)
        if k >= n_blk:
            u_blk = k - n_blk
            order.extend(spatial[u_blk * groups_per_blk:(u_blk + 1) * groups_per_blk])
    order.extend([functools.partial(out_unit, c0) for c0 in range(0, D_MODEL, OUT_BLK)])
    order.append(last)
    for unit in order:
        unit()


def _cmlp_layer(x2d, w, *, tm, blk, emit_v):
    n = x2d.shape[0]
    row_spec = pl.BlockSpec((tm, D_MODEL), lambda i: (i, 0))
    consts = (w["w_in"], w["b_in"], w["ln_g"], w["ln_b"], w["w_s"], w["b_s"], w["w_out"], w["ln1_g"], w["ln1_b"])
    out_shape = [jax.ShapeDtypeStruct(x2d.shape, F32)]
    out_specs = [row_spec]
    if emit_v:
        out_shape.append(jax.ShapeDtypeStruct((n, CM_WIDTH), F32))
        out_specs.append(pl.BlockSpec((tm, CM_WIDTH), lambda i: (i, 0)))
    return pl.pallas_call(
        functools.partial(_cmlp_kernel, blk=blk, emit_v=emit_v),
        grid=(n // tm,),
        in_specs=[row_spec] + [_const_spec(c.shape) for c in consts],
        out_specs=out_specs,
        out_shape=out_shape,
        scratch_shapes=[pltpu.VMEM((tm, CM_WIDTH), F32),
                        pltpu.VMEM((tm, CM_WIDTH), F32),
                        pltpu.VMEM((tm, CM_WIDTH), BF16),
                        pltpu.VMEM((tm, CM_WIDTH), BF16)],
        compiler_params=pltpu.CompilerParams(dimension_semantics=("arbitrary",), vmem_limit_bytes=VMEM_LIMIT),
        name="cmlp_v" if emit_v else "cmlp",
    )(x2d, *consts)


def kernel(x_prompt, x_sample, state_ssm, state_conv, ssd_w_in, ssd_conv_w, ssd_conv_b, ssd_dt_bias, ssd_a_log, ssd_d, ssd_norm_w, ssd_w_out, cm_w_in, cm_b_in, cm_ln_g, cm_ln_b, cm_w_s, cm_b_s, cm_w_out, ffn_w1, ffn_w2, ln1_g, ln1_b, ln2_g, ln2_b):
    row = lambda a: a.reshape(1, -1)
    perm = jnp.asarray(HEAD_PERM)

    def head_lanes(a):
        return jnp.pad(a[perm], (0, LANES - SSD_HEADS)).reshape(1, LANES)

    w_in = ssd_w_in[0]
    zx_end = SSD_D_INNER + SSD_CONV_DIM
    ssd_w = {
        "w_zx": w_in.astype(BF16),
        "w_dt": jnp.pad(w_in[:, zx_end:][:, perm], ((0, 0), (0, LANES - SSD_HEADS))).astype(BF16),
        "conv_w": ssd_conv_w[0],
        "conv_b": row(ssd_conv_b[0]),
        "dt_bias": head_lanes(ssd_dt_bias[0]),
        "a_log": head_lanes(ssd_a_log[0]),
        "d_exp": row(jnp.repeat(ssd_d[0], SSD_HEAD_DIM)),
        "norm_w": row(ssd_norm_w[0]),
        "w_out": ssd_w_out[0].astype(BF16),
        "e2": jnp.asarray(_expand_matrix(), BF16),
        "ln_g": row(ln1_g[0]),
        "ln_b": row(ln1_b[0]),
    }
    cm_w = {
        "w_in": cm_w_in[0].astype(BF16),
        "b_in": row(cm_b_in[0]),
        "ln_g": row(cm_ln_g[0]),
        "ln_b": row(cm_ln_b[0]),
        "w_s": cm_w_s[0],
        "b_s": jnp.repeat(cm_b_s[0].T, CM_GROUP_W, axis=1),
        "w_out": cm_w_out[0].astype(BF16),
        "ln1_g": row(ln1_g[1]),
        "ln1_b": row(ln1_b[1]),
    }
    ffn_w1b, ffn_w2b = ffn_w1.astype(BF16), ffn_w2.astype(BF16)
    ffn = [functools.partial(_ffn_layer, w1=ffn_w1b, w2=ffn_w2b, g=row(ln2_g[i]), b=row(ln2_b[i]), layer=i, tm=512)
           for i in range(DEPTH)]

    def trunk(x, conv_prev, h0, *, sb_n, tt, blk, emit_v):
        shape = x.shape
        x1, conv_new, h_new = _ssd_layer(x, conv_prev, h0, ssd_w, sb_n=sb_n, tt=tt)
        x2 = ffn[0](x1.reshape(-1, D_MODEL))
        res = _cmlp_layer(x2, cm_w, tm=512, blk=blk, emit_v=emit_v)
        x4 = ffn[1](res[0])
        v_rows = res[1].reshape(shape[0], shape[1], CM_WIDTH) if emit_v else None
        return x4.reshape(shape), conv_new, h_new, v_rows

    y_p, conv_p, h_p, _ = trunk(x_prompt, None, None, sb_n=1, tt=256, blk=CM_BLOCK, emit_v=False)
    y_s, conv_s, h_s, v_s = trunk(x_sample, state_conv[0], state_ssm[0], sb_n=2, tt=CHUNK,
                                  blk=x_sample.shape[1], emit_v=True)
    return (y_p, y_s, h_p[None], conv_p[None], h_s[None], conv_s[None], v_s[None])
```

```python
import functools

import numpy as np
import jax
import jax.numpy as jnp
from jax import lax
from jax.experimental import pallas as pl
from jax.experimental.pallas import tpu as pltpu

F32 = jnp.float32
BF16 = jnp.bfloat16

D_MODEL = 1024
DEPTH = 2
CHUNK = 64
SSD_D_INNER = 2 * D_MODEL
SSD_HEAD_DIM = 64
SSD_HEADS = SSD_D_INNER // SSD_HEAD_DIM
SSD_GROUPS = 8
SSD_STATE = 128
SSD_CONV = 4
SSD_BC = SSD_GROUPS * SSD_STATE
SSD_CONV_DIM = SSD_D_INNER + 2 * SSD_BC
HEADS_PER_GROUP = SSD_HEADS // SSD_GROUPS
GROUP_W = HEADS_PER_GROUP * SSD_HEAD_DIM
NORM_GROUP_W = SSD_D_INNER // SSD_GROUPS
CM_BLOCK = 128
CM_WIDTH = 2 * D_MODEL
CM_GROUPS = 8
CM_GROUP_W = CM_WIDTH // CM_GROUPS
FFN_HIDDEN = 4 * D_MODEL
DN_ALPHA = (2 * DEPTH) ** 0.25
LN_EPS = 1e-5
LOG2_E = 1.4426950408889634
GELU_K = 0.7978845608028654
GELU_C = 0.044715

LANES = 128
CONV_PAD = 8
COL_BLK = 256
OUT_BLK = 256
COL_BLK_CM = 256
SCAN_UNIT_GROUPS = 4
CONVS_BESIDE_SCANS_PER_CHUNK4 = 6
VMEM_LIMIT = 56 * 1024 * 1024
SSD_PROMPT_TILE = (1, 256)
SSD_SAMPLE_TILE = (2, CHUNK)
ROW_TILE = 512

HEAD_PERM = np.concatenate([np.arange(0, SSD_HEADS, 2), np.arange(1, SSD_HEADS, 2)])


def _expand_matrix():
    e = np.zeros((2 * LANES, SSD_D_INNER), np.float32)
    for k, h in enumerate(HEAD_PERM):
        e[k, h * SSD_HEAD_DIM:(h + 1) * SSD_HEAD_DIM] = 1.0
        e[LANES + k, h * SSD_HEAD_DIM:(h + 1) * SSD_HEAD_DIM] = 1.0
    return e


def _layer_norm(x, g, b):
    mu = jnp.mean(x, axis=-1, keepdims=True)
    xc = x - mu
    var = jnp.mean(xc * xc, axis=-1, keepdims=True)
    return xc * lax.rsqrt(var + LN_EPS) * g + b


def _silu(x):
    half = 0.5 * x
    return half + half * jnp.tanh(half)


def _split2(v):
    hi = v.astype(BF16)
    lo = (v - hi.astype(F32)).astype(BF16)
    return jnp.concatenate([hi, lo], axis=1)


def _const_spec(shape):
    nd = len(shape)
    return pl.BlockSpec(shape, lambda *_: (0,) * nd, pipeline_mode=pl.Buffered(1))


def _ssd_stage_a(x_ref, wz_ref, wdt_ref, cw_ref, cb_ref, ext_s, handoff, *, sb_n, tt):
    assert SSD_CONV == 4
    z_w, xs_w, bc_w, dtr_w = handoff
    rows = sb_n * tt
    hist = CONV_PAD - (SSD_CONV - 1)
    live = {}

    def first():
        live["xb"] = x_ref[...].reshape(rows, D_MODEL).astype(BF16)
        dtr_w[...] = jnp.dot(live["xb"], wdt_ref[...], preferred_element_type=F32)

    def dot_unit(c0):
        cs = slice(c0, c0 + COL_BLK)
        raw = jnp.dot(live["xb"], wz_ref[:, SSD_D_INNER + c0:SSD_D_INNER + c0 + COL_BLK],
                      preferred_element_type=F32)
        for sb in range(sb_n):
            ext_s[sb, CONV_PAD:CONV_PAD + tt, cs] = raw[sb * tt:(sb + 1) * tt]

    def conv_unit(c0):
        cs = slice(c0, c0 + COL_BLK)
        for sb in range(sb_n):
            ext = ext_s[sb, :, cs]
            back1 = pltpu.roll(ext, 1, axis=0)
            older = ext * cw_ref[1:2, cs] + back1 * cw_ref[0:1, cs]
            acc = (cb_ref[:, cs] + ext * cw_ref[3:4, cs] + back1 * cw_ref[2:3, cs]
                   + pltpu.roll(older, 2, axis=0))[CONV_PAD:]
            act = _silu(acc)
            rs = slice(sb * tt, (sb + 1) * tt)
            if c0 < SSD_D_INNER:
                xs_w[rs, cs] = act
            else:
                bc_w[rs, c0 - SSD_D_INNER:c0 - SSD_D_INNER + COL_BLK] = act.astype(BF16)
            ext_s[sb, hist:CONV_PAD, cs] = ext[tt + hist:tt + CONV_PAD]

    def z_unit(c0):
        z_w[:, c0:c0 + COL_BLK] = jnp.dot(live["xb"], wz_ref[:, c0:c0 + COL_BLK], preferred_element_type=F32)

    blocks = range(0, SSD_CONV_DIM, COL_BLK)
    return (first,
            [functools.partial(dot_unit, c0) for c0 in blocks],
            [functools.partial(conv_unit, c0) for c0 in blocks],
            [functools.partial(z_unit, c0) for c0 in range(0, SSD_D_INNER, COL_BLK)])


def _ssd_stage_b(handoff, dtb_ref, alog_ref, dexp_ref, nw_ref, e2_ref, yb_w,
                 dtx_s, acx_s, y_s, ac_s, h_s, *, sb_n, tt):
    z_r, xs_r, bc_r, dtr_r = handoff
    rows = sb_n * tt
    nch = tt // CHUNK
    n_pairs = SSD_HEADS // 2
    live = {}

    def prologue():
        v = dtr_r[...] + dtb_ref[...]
        dt = jnp.maximum(v, 0.0) + jnp.log1p(jnp.exp(-jnp.abs(v)))
        a_neg = -jnp.exp(alog_ref[...])
        d_a = dt * a_neg
        ri = lax.broadcasted_iota(jnp.int32, (rows, rows), 0)
        ci = lax.broadcasted_iota(jnp.int32, (rows, rows), 1)
        same_chunk = (ri & -CHUNK) == (ci & -CHUNK)
        tril = jnp.where((ci <= ri) & same_chunk, 1.0, 0.0).astype(BF16)
        cs = jnp.dot(tril, _split2(d_a), preferred_element_type=F32)
        acum = (cs[:, 0:LANES] + cs[:, LANES:]) * LOG2_E
        ac_s[...] = acum
        acx_s[...] = jnp.dot(_split2(acum), e2_ref[...], preferred_element_type=F32)
        dtx_s[...] = xs_r[...] * jnp.dot(_split2(dt), e2_ref[...], preferred_element_type=F32)
        row_i = lax.broadcasted_iota(jnp.int32, (CHUNK, LANES), 0)
        lane_i = lax.broadcasted_iota(jnp.int32, (CHUNK, LANES), 1)
        live["causal2"] = (lane_i & (CHUNK - 1)) <= row_i
        live["first_half"] = lane_i < CHUNK

    def scan_unit(chunk, g0):
        sb = chunk // nch
        r0 = chunk * CHUNK
        rs = slice(r0, r0 + CHUNK)
        if g0 == 0:
            ac_t = ac_s[rs, :].T
            live["pair_rows"] = jnp.concatenate([ac_t[0:n_pairs], ac_t[n_pairs:2 * n_pairs]], axis=1)
        pair_rows, causal2, first_half = live["pair_rows"], live["causal2"], live["first_half"]
        groups = range(g0, g0 + SCAN_UNIT_GROUPS)
        b, c, h, acx, cb2, y_off = {}, {}, {}, {}, {}, {}
        for g in groups:
            gs = slice(g * GROUP_W, (g + 1) * GROUP_W)
            b[g] = bc_r[rs, g * SSD_STATE:(g + 1) * SSD_STATE]
            c[g] = bc_r[rs, SSD_BC + g * SSD_STATE:SSD_BC + (g + 1) * SSD_STATE]
            bb = jnp.concatenate([b[g], b[g]], axis=0)
            cb2[g] = lax.dot_general(c[g], bb, (((1,), (1,)), ((), ())), preferred_element_type=F32)
            h[g] = h_s[sb, g]
            acx[g] = acx_s[rs, gs]
            y_off[g] = jnp.dot(c[g], h[g].astype(BF16), preferred_element_type=F32)
        y_diag = {}
        for g in groups:
            for jj in range(2):
                j = 2 * g + jj
                ls = slice(j * LANES, (j + 1) * LANES)
                seg = acx_s[rs, ls] - pair_rows[j:j + 1, :]
                decay = jnp.where(causal2, jnp.exp2(seg), 0.0)
                m2 = (cb2[g] * decay).astype(BF16)
                xp = dtx_s[rs, ls]
                xbd = jnp.concatenate([jnp.where(first_half, xp, 0.0), jnp.where(first_half, 0.0, xp)], axis=0)
                y_diag[j] = jnp.dot(m2, xbd.astype(BF16), preferred_element_type=F32)
        upd, a_last = {}, {}
        for g in groups:
            gs = slice(g * GROUP_W, (g + 1) * GROUP_W)
            a_last[g] = acx_s[r0 + CHUNK - 1:r0 + CHUNK, gs]
            xw = (dtx_s[rs, gs] * jnp.exp2(a_last[g] - acx[g])).astype(BF16)
            upd[g] = lax.dot_general(b[g], xw, (((0,), (0,)), ((), ())), preferred_element_type=F32)
        for g in groups:
            y_g = jnp.concatenate([y_diag[2 * g], y_diag[2 * g + 1]], axis=1) + y_off[g] * jnp.exp2(acx[g])
            y_s[rs, g * GROUP_W:(g + 1) * GROUP_W] = y_g
            h_s[sb, g] = h[g] * jnp.exp2(a_last[g]) + upd[g]

    def gate_unit(g):
        gs = slice(g * NORM_GROUP_W, (g + 1) * NORM_GROUP_W)
        y = y_s[:, gs] + dexp_ref[:, gs] * xs_r[:, gs]
        zg = z_r[:, gs]
        gg = y * _silu(zg)
        ms = jnp.mean(gg * gg, axis=-1, keepdims=True)
        yb_w[:, gs] = (gg * lax.rsqrt(ms + LN_EPS) * nw_ref[:, gs]).astype(BF16)

    return (prologue,
            [functools.partial(scan_unit, c, g) for c in range(sb_n * nch)
             for g in range(0, SSD_GROUPS, SCAN_UNIT_GROUPS)],
            [functools.partial(gate_unit, g) for g in range(SSD_GROUPS)])


def _ssd_stage_c(yb_r, x_ref, wout_ref, g_ref, b_ref, o_ref, *, sb_n, tt):
    rows = sb_n * tt

    def out_unit(c0):
        mix = jnp.dot(yb_r[...], wout_ref[:, c0:c0 + OUT_BLK], preferred_element_type=F32)
        o_ref[:, :, c0:c0 + OUT_BLK] = mix.reshape(sb_n, tt, OUT_BLK)

    def last():
        mix = o_ref[...].reshape(rows, D_MODEL)
        x = x_ref[...].reshape(rows, D_MODEL)
        o_ref[...] = _layer_norm(DN_ALPHA * x + mix, g_ref[...], b_ref[...]).reshape(sb_n, tt, D_MODEL)

    return [functools.partial(out_unit, c0) for c0 in range(0, D_MODEL, OUT_BLK)], last


def _ssd_trace_order(stage_a, stage_b, stage_c, state_units=()):
    a_first, a_dots, a_convs, a_z = stage_a
    b_pro, b_scans, b_gates = stage_b
    c_units = list(stage_c[0]) + [stage_c[1]]
    lead = 4
    n_blk = len(a_dots)
    order = [a_first] + list(state_units[:1]) + a_dots[0:2] + [b_pro] + a_dots[2:lead]
    nxt_conv, nxt_dot = 0, lead

    def conv_and_dot():
        nonlocal nxt_conv, nxt_dot
        order.append(a_convs[nxt_conv])
        nxt_conv += 1
        if nxt_dot < n_blk:
            order.append(a_dots[nxt_dot])
            nxt_dot += 1

    n_scan = len(b_scans)
    n_chunks = n_scan * SCAN_UNIT_GROUPS // SSD_GROUPS
    n_beside_scans = CONVS_BESIDE_SCANS_PER_CHUNK4 * n_chunks // 4
    n_c = len(c_units)
    for u, scan in enumerate(b_scans):
        order.append(scan)
        for _ in range((u + 1) * n_beside_scans // n_scan - u * n_beside_scans // n_scan):
            conv_and_dot()
        for _ in range((u + 1) * n_c // n_scan - u * n_c // n_scan):
            order.append(c_units.pop(0))
    for g, gate in enumerate(b_gates):
        order.append(gate)
        order.extend(a_z[g:g + 1])
        if nxt_conv < n_blk:
            conv_and_dot()
    order.extend(a_z[len(b_gates):])
    while nxt_conv < n_blk:
        conv_and_dot()
    order.extend(c_units)
    n_units = (1 + 2 * n_blk + len(a_z) + 1 + len(b_scans) + len(b_gates) + len(stage_c[0]) + 1
               + len(state_units))
    assert len(order) == n_units and len(set(map(id, order))) == n_units
    return order


def _ssd_kernel(*refs, sb_n, tt, nt, n_tiles, has_state):
    if has_state:
        (x_ref, xres_ref, cprev_ref, h0_ref, *refs) = refs
    else:
        (x_ref, xres_ref, *refs) = refs
    (wz_ref, wdt_ref, cw_ref, cb_ref, dtb_ref, alog_ref, dexp_ref, nw_ref, wout_ref, e2_ref,
     g_ref, b_ref, o_ref, cnew_ref, hout_ref,
     ext_s, dtx_s, acx_s, y_s, ac_s, h_s, yb0_s, yb1_s, *handoffs) = refs
    n_hand = len(handoffs) // 2
    hand = (tuple(handoffs[:n_hand]), tuple(handoffs[n_hand:]))
    yb = (yb0_s, yb1_s)
    i = pl.program_id(0)
    t_a = jnp.minimum(i, n_tiles - 1) % nt
    t_b = jnp.clip(i - 1, 0, n_tiles - 1) % nt
    hist = CONV_PAD - (SSD_CONV - 1)

    @pl.when(i == 0)
    def _zero_first_handoff():
        for ref in hand[1] + (yb[0],):
            ref[...] = jnp.zeros(ref.shape, ref.dtype)

    @pl.when(t_a == 0)
    def _conv_history():
        ext_s[:, 0:CONV_PAD, :] = jnp.zeros((sb_n, CONV_PAD, SSD_CONV_DIM), F32)
        if has_state:
            for sb in range(sb_n):
                ext_s[sb, hist:CONV_PAD, :] = cprev_ref[sb]

    def state_in():
        if has_state:
            for sb in range(sb_n):
                for g in range(SSD_GROUPS):
                    hg = h0_ref[sb, g * HEADS_PER_GROUP:(g + 1) * HEADS_PER_GROUP]
                    h_s[sb, g] = hg.reshape(GROUP_W, SSD_STATE).T
        else:
            h_s[...] = jnp.zeros(h_s.shape, F32)

    inline_state_in = has_state and nt == 1
    state_units = (state_in,) if inline_state_in else ()
    if not inline_state_in:
        pl.when(t_b == 0)(state_in)

    def all_stages(parity):
        stage_a = _ssd_stage_a(x_ref, wz_ref, wdt_ref, cw_ref, cb_ref, ext_s, hand[parity], sb_n=sb_n, tt=tt)
        stage_b = _ssd_stage_b(hand[1 - parity], dtb_ref, alog_ref, dexp_ref, nw_ref, e2_ref, yb[1 - parity],
                               dtx_s, acx_s, y_s, ac_s, h_s, sb_n=sb_n, tt=tt)
        stage_c = _ssd_stage_c(yb[parity], xres_ref, wout_ref, g_ref, b_ref, o_ref, sb_n=sb_n, tt=tt)
        for unit in _ssd_trace_order(stage_a, stage_b, stage_c, state_units):
            unit()

    @pl.when(i % 2 == 0)
    def _even():
        all_stages(0)

    @pl.when(i % 2 == 1)
    def _odd():
        all_stages(1)

    @pl.when((t_a == nt - 1) & (i < n_tiles))
    def _conv_out():
        cnew_ref[...] = ext_s[:, hist:CONV_PAD, :]

    @pl.when((t_b == nt - 1) & (i >= 1) & (i <= n_tiles))
    def _state_out():
        for sb in range(sb_n):
            for g in range(SSD_GROUPS):
                hout_ref[sb, g * HEADS_PER_GROUP:(g + 1) * HEADS_PER_GROUP] = (
                    h_s[sb, g].T.reshape(HEADS_PER_GROUP, SSD_HEAD_DIM, SSD_STATE))


def _ssd_layer(x, conv_prev, h0, w, *, sb_n, tt):
    nseq, seq_len, _ = x.shape
    has_state = h0 is not None
    rows = sb_n * tt
    nt = seq_len // tt
    n_tiles = (nseq // sb_n) * nt

    def tile_a(i):
        return jnp.minimum(i, n_tiles - 1)

    def tile_b(i):
        return jnp.clip(i - 1, 0, n_tiles - 1)

    def tile_c(i):
        return jnp.maximum(i - 2, 0)

    x_spec = pl.BlockSpec((sb_n, tt, D_MODEL), lambda i: (tile_a(i) // nt, tile_a(i) % nt, 0))
    o_spec = pl.BlockSpec((sb_n, tt, D_MODEL), lambda i: (tile_c(i) // nt, tile_c(i) % nt, 0))
    conv_spec = pl.BlockSpec((sb_n, SSD_CONV - 1, SSD_CONV_DIM), lambda i: (tile_a(i) // nt, 0, 0))
    state_spec = pl.BlockSpec((sb_n, SSD_HEADS, SSD_HEAD_DIM, SSD_STATE), lambda i: (tile_b(i) // nt, 0, 0, 0))
    consts = (w["w_zx"], w["w_dt"], w["conv_w"], w["conv_b"], w["dt_bias"], w["a_log"], w["d_exp"],
              w["norm_w"], w["w_out"], w["e2"], w["ln_g"], w["ln_b"])
    in_specs = ([x_spec, o_spec] + ([conv_spec, state_spec] if has_state else [])
                + [_const_spec(c.shape) for c in consts])
    args = (x, x) + ((conv_prev, h0) if has_state else ()) + consts
    handoff = [
        pltpu.VMEM((rows, SSD_D_INNER), F32),
        pltpu.VMEM((rows, SSD_D_INNER), F32),
        pltpu.VMEM((rows, 2 * SSD_BC), BF16),
        pltpu.VMEM((rows, LANES), F32),
    ]
    scratch = [
        pltpu.VMEM((sb_n, tt + CONV_PAD, SSD_CONV_DIM), F32),
        pltpu.VMEM((rows, SSD_D_INNER), F32),
        pltpu.VMEM((rows, SSD_D_INNER), F32),
        pltpu.VMEM((rows, SSD_D_INNER), F32),
        pltpu.VMEM((rows, LANES), F32),
        pltpu.VMEM((sb_n, SSD_GROUPS, SSD_STATE, GROUP_W), F32),
        pltpu.VMEM((rows, SSD_D_INNER), BF16),
        pltpu.VMEM((rows, SSD_D_INNER), BF16),
    ] + handoff + handoff
    out_shape = (
        jax.ShapeDtypeStruct(x.shape, F32),
        jax.ShapeDtypeStruct((nseq, SSD_CONV - 1, SSD_CONV_DIM), F32),
        jax.ShapeDtypeStruct((nseq, SSD_HEADS, SSD_HEAD_DIM, SSD_STATE), F32),
    )
    return pl.pallas_call(
        functools.partial(_ssd_kernel, sb_n=sb_n, tt=tt, nt=nt, n_tiles=n_tiles, has_state=has_state),
        grid=(n_tiles + 2,),
        in_specs=in_specs,
        out_specs=(o_spec, conv_spec, state_spec),
        out_shape=out_shape,
        scratch_shapes=scratch,
        compiler_params=pltpu.CompilerParams(dimension_semantics=("arbitrary",), vmem_limit_bytes=VMEM_LIMIT),
        name="ssd_state" if has_state else "ssd_fresh",
    )(*args)


def _ffn_kernel(x_ref, w1_ref, w2_ref, g_ref, b_ref, o_ref, sum_s, *, n_tiles):
    i = pl.program_id(0)

    def mlp():
        x = x_ref[...]
        h = jnp.dot(x.astype(BF16), w1_ref[...], preferred_element_type=F32)
        h = jnp.maximum(h, 0.0)
        y = jnp.dot((h * h).astype(BF16), w2_ref[...], preferred_element_type=F32)
        sum_s[...] = DN_ALPHA * x + y

    def norm():
        o_ref[...] = _layer_norm(sum_s[...], g_ref[...], b_ref[...])

    @pl.when(i == 0)
    def _first():
        mlp()

    @pl.when((i > 0) & (i < n_tiles))
    def _steady():
        norm()
        mlp()

    @pl.when(i == n_tiles)
    def _last():
        norm()


def _ffn_layer(x2d, w1, w2, g, b, *, layer, tm):
    n_tiles = x2d.shape[0] // tm
    x_spec = pl.BlockSpec((tm, D_MODEL), lambda i: (jnp.minimum(i, n_tiles - 1), 0))
    o_spec = pl.BlockSpec((tm, D_MODEL), lambda i: (jnp.maximum(i - 1, 0), 0))

    def layer_spec(shape):
        return pl.BlockSpec((None,) + shape[1:], lambda i: (layer, 0, 0), pipeline_mode=pl.Buffered(1))

    return pl.pallas_call(
        functools.partial(_ffn_kernel, n_tiles=n_tiles),
        grid=(n_tiles + 1,),
        in_specs=[x_spec, layer_spec(w1.shape), layer_spec(w2.shape), _const_spec(g.shape), _const_spec(b.shape)],
        out_specs=o_spec,
        out_shape=jax.ShapeDtypeStruct(x2d.shape, F32),
        scratch_shapes=[pltpu.VMEM((tm, D_MODEL), F32)],
        compiler_params=pltpu.CompilerParams(dimension_semantics=("arbitrary",), vmem_limit_bytes=VMEM_LIMIT),
        name="ffn",
    )(x2d, w1, w2, g, b)


def _cmlp_kernel(x_ref, win_ref, bin_ref, lng_ref, lnb_ref, ws_ref, bs_ref, wout_ref, g_ref, b_ref,
                 o_ref, *rest, blk, emit_v):
    if emit_v:
        v_ref, u_s, vpre_s, v_s, ub_s = rest
    else:
        u_s, vpre_s, v_s, ub_s = rest
    tm = x_ref.shape[0]
    xb = x_ref[...].astype(BF16)

    def dot_unit(dst, c_src, c_dst):
        dst[:, c_dst:c_dst + COL_BLK_CM] = jnp.dot(xb, win_ref[:, c_src:c_src + COL_BLK_CM],
                                                   preferred_element_type=F32)

    def gelu_unit(dst, c_src, c_dst):
        h = dst[:, c_dst:c_dst + COL_BLK_CM] + bin_ref[:, c_src:c_src + COL_BLK_CM]
        half = 0.5 * h
        inner = h * (GELU_K + (GELU_K * GELU_C) * (h * h))
        dst[:, c_dst:c_dst + COL_BLK_CM] = half + half * jnp.tanh(inner)

    def norm_v():
        v = _layer_norm(vpre_s[...], lng_ref[...], lnb_ref[...])
        if emit_v:
            v_ref[...] = v
        v_s[...] = v.astype(BF16)

    def spatial_unit(g):
        ri = lax.broadcasted_iota(jnp.int32, (blk, blk), 0)
        ci = lax.broadcasted_iota(jnp.int32, (blk, blk), 1)
        gs = slice(g * CM_GROUP_W, (g + 1) * CM_GROUP_W)
        ws_g = jnp.where(ci <= ri, ws_ref[g, 0:blk, 0:blk], 0.0).astype(BF16)
        for i in range(tm // blk):
            rs = slice(i * blk, (i + 1) * blk)
            s = jnp.dot(ws_g, v_s[rs, gs], preferred_element_type=F32) + bs_ref[0:blk, gs]
            ub_s[rs, gs] = (u_s[rs, gs] * s).astype(BF16)

    def out_unit(c0):
        o_ref[:, c0:c0 + OUT_BLK] = jnp.dot(ub_s[...], wout_ref[:, c0:c0 + OUT_BLK],
                                            preferred_element_type=F32)

    def last():
        o_ref[...] = _layer_norm(DN_ALPHA * x_ref[...] + o_ref[...], g_ref[...], b_ref[...])

    n_blk = CM_WIDTH // COL_BLK_CM
    dots = ([functools.partial(dot_unit, vpre_s, CM_WIDTH + k * COL_BLK_CM, k * COL_BLK_CM) for k in range(n_blk)]
            + [functools.partial(dot_unit, u_s, k * COL_BLK_CM, k * COL_BLK_CM) for k in range(n_blk)])
    gelus = ([functools.partial(gelu_unit, vpre_s, CM_WIDTH + k * COL_BLK_CM, k * COL_BLK_CM) for k in range(n_blk)]
             + [functools.partial(gelu_unit, u_s, k * COL_BLK_CM, k * COL_BLK_CM) for k in range(n_blk)])
    spatial = [functools.partial(spatial_unit, g) for g in range(CM_GROUPS)]
    groups_per_blk = COL_BLK_CM // CM_GROUP_W
    order = [dots[0]]
    for k in range(2 * n_blk):
        order.extend(dots[k + 1:k + 2])
        order.append(gelus[k])
        if k == n_blk - 1:
            order.append(norm_v)
        if k >= n_blk:
            u_blk = k - n_blk
            order.extend(spatial[u_blk * groups_per_blk:(u_blk + 1) * groups_per_blk])
    order.extend([functools.partial(out_unit, c0) for c0 in range(0, D_MODEL, OUT_BLK)])
    order.append(last)
    for unit in order:
        unit()


def _cmlp_layer(x2d, w, *, tm, blk, emit_v):
    n = x2d.shape[0]
    row_spec = pl.BlockSpec((tm, D_MODEL), lambda i: (i, 0))
    consts = (w["w_in"], w["b_in"], w["ln_g"], w["ln_b"], w["w_s"], w["b_s"], w["w_out"], w["ln1_g"], w["ln1_b"])
    out_shape = [jax.ShapeDtypeStruct(x2d.shape, F32)]
    out_specs = [row_spec]
    if emit_v:
        out_shape.append(jax.ShapeDtypeStruct((n, CM_WIDTH), F32))
        out_specs.append(pl.BlockSpec((tm, CM_WIDTH), lambda i: (i, 0)))
    return pl.pallas_call(
        functools.partial(_cmlp_kernel, blk=blk, emit_v=emit_v),
        grid=(n // tm,),
        in_specs=[row_spec] + [_const_spec(c.shape) for c in consts],
        out_specs=out_specs,
        out_shape=out_shape,
        scratch_shapes=[pltpu.VMEM((tm, CM_WIDTH), F32),
                        pltpu.VMEM((tm, CM_WIDTH), F32),
                        pltpu.VMEM((tm, CM_WIDTH), BF16),
                        pltpu.VMEM((tm, CM_WIDTH), BF16)],
        compiler_params=pltpu.CompilerParams(dimension_semantics=("arbitrary",), vmem_limit_bytes=VMEM_LIMIT),
        name="cmlp_v" if emit_v else "cmlp",
    )(x2d, *consts)


def kernel(x_prompt, x_sample, state_ssm, state_conv, ssd_w_in, ssd_conv_w, ssd_conv_b, ssd_dt_bias, ssd_a_log, ssd_d, ssd_norm_w, ssd_w_out, cm_w_in, cm_b_in, cm_ln_g, cm_ln_b, cm_w_s, cm_b_s, cm_w_out, ffn_w1, ffn_w2, ln1_g, ln1_b, ln2_g, ln2_b):
    row = lambda a: a.reshape(1, -1)
    perm = jnp.asarray(HEAD_PERM)

    def head_lanes(a):
        return jnp.pad(a[perm], (0, LANES - SSD_HEADS)).reshape(1, LANES)

    w_in = ssd_w_in[0]
    zx_end = SSD_D_INNER + SSD_CONV_DIM
    ssd_w = {
        "w_zx": w_in.astype(BF16),
        "w_dt": jnp.pad(w_in[:, zx_end:][:, perm], ((0, 0), (0, LANES - SSD_HEADS))).astype(BF16),
        "conv_w": ssd_conv_w[0],
        "conv_b": row(ssd_conv_b[0]),
        "dt_bias": head_lanes(ssd_dt_bias[0]),
        "a_log": head_lanes(ssd_a_log[0]),
        "d_exp": row(jnp.repeat(ssd_d[0], SSD_HEAD_DIM)),
        "norm_w": row(ssd_norm_w[0]),
        "w_out": ssd_w_out[0].astype(BF16),
        "e2": jnp.asarray(_expand_matrix(), BF16),
        "ln_g": row(ln1_g[0]),
        "ln_b": row(ln1_b[0]),
    }
    cm_w = {
        "w_in": cm_w_in[0].astype(BF16),
        "b_in": row(cm_b_in[0]),
        "ln_g": row(cm_ln_g[0]),
        "ln_b": row(cm_ln_b[0]),
        "w_s": cm_w_s[0],
        "b_s": jnp.repeat(cm_b_s[0].T, CM_GROUP_W, axis=1),
        "w_out": cm_w_out[0].astype(BF16),
        "ln1_g": row(ln1_g[1]),
        "ln1_b": row(ln1_b[1]),
    }
    ffn_w1b, ffn_w2b = ffn_w1.astype(BF16), ffn_w2.astype(BF16)
    ffn = [functools.partial(_ffn_layer, w1=ffn_w1b, w2=ffn_w2b, g=row(ln2_g[i]), b=row(ln2_b[i]), layer=i, tm=ROW_TILE)
           for i in range(DEPTH)]

    def trunk(x, conv_prev, h0, *, ssd_tile, blk, emit_v):
        shape = x.shape
        x1, conv_new, h_new = _ssd_layer(x, conv_prev, h0, ssd_w, sb_n=ssd_tile[0], tt=ssd_tile[1])
        x2 = ffn[0](x1.reshape(-1, D_MODEL))
        res = _cmlp_layer(x2, cm_w, tm=ROW_TILE, blk=blk, emit_v=emit_v)
        x4 = ffn[1](res[0])
        v_rows = res[1].reshape(shape[0], shape[1], CM_WIDTH) if emit_v else None
        return x4.reshape(shape), conv_new, h_new, v_rows

    y_p, conv_p, h_p, _ = trunk(x_prompt, None, None, ssd_tile=SSD_PROMPT_TILE, blk=CM_BLOCK, emit_v=False)
    y_s, conv_s, h_s, v_s = trunk(x_sample, state_conv[0], state_ssm[0], ssd_tile=SSD_SAMPLE_TILE,
                                  blk=x_sample.shape[1], emit_v=True)
    return (y_p, y_s, h_p[None], conv_p[None], h_s[None], conv_s[None], v_s[None])
```
